```python
import math, functools
import jax, jax.numpy as jnp
from jax import lax
import numpy as np

D_MODEL = 1024
BATCH = 8
SEQ = 2048
DEPTH = 4
DEC_BATCH = 128
DEC_SEQ = 1
PAST_LEN = 2048
PAGE_SIZE = 128

N_A_LAYERS = DEPTH // 2
N_B_LAYERS = DEPTH - N_A_LAYERS
SSM_EXPAND = 2
SSM_WIDTH = SSM_EXPAND * D_MODEL
SSM_GROUP = 16
SSM_GROUPS = SSM_WIDTH // SSM_GROUP
SSM_STATE = 64
SSM_CHUNK = 128
DT_MIN = 0.001
DT_MAX = 0.1
HEAD_DIM = 64
N_HEADS = D_MODEL // HEAD_DIM
ATTN_WIDTH = N_HEADS * HEAD_DIM
DILATION_GROUPS = ((128, 1), (512, 4), (2048, 16))
N_GROUPS = len(DILATION_GROUPS)
MAX_WINDOW = max(w for w, _ in DILATION_GROUPS)
ALIBI_MAX_EXP = 8.0
RMS_EPS = 1e-6
NEG_INF = -1e30

kernel_name = 'yoco_s5_dilated_alibi_step'


def rms_norm(x, g):
    xf = x.astype(jnp.float32)
    y = xf * lax.rsqrt(jnp.mean(xf * xf, axis=-1, keepdims=True) + RMS_EPS)
    return (y * g.astype(jnp.float32)).astype(x.dtype)


def alibi_slopes():
    n = N_GROUPS * N_HEADS
    e = jnp.arange(1, n + 1, dtype=jnp.float32)
    return jnp.exp2(-ALIBI_MAX_EXP * e / n).reshape(N_GROUPS, N_HEADS)


def s5_discretize(lam_re, lam_im, log_dt, b_re, b_im):
    lam_re = lam_re.astype(jnp.float32)
    lam_im = lam_im.astype(jnp.float32)
    dt = jnp.exp(log_dt.astype(jnp.float32))[:, None]
    mag = jnp.exp(lam_re * dt)
    ab_re = mag * jnp.cos(lam_im * dt)
    ab_im = mag * jnp.sin(lam_im * dt)
    den = lam_re * lam_re + lam_im * lam_im
    num_re = ab_re - 1.0
    coef_re = (num_re * lam_re + ab_im * lam_im) / den
    coef_im = (ab_im * lam_re - num_re * lam_im) / den
    b_re = b_re.astype(jnp.float32)
    b_im = b_im.astype(jnp.float32)
    bb_re = coef_re[..., None] * b_re - coef_im[..., None] * b_im
    bb_im = coef_re[..., None] * b_im + coef_im[..., None] * b_re
    return ab_re, ab_im, bb_re, bb_im


def _complex_affine_combine(e1, e2):
    a1r, a1i, b1r, b1i = e1
    a2r, a2i, b2r, b2i = e2
    return (a2r * a1r - a2i * a1i,
            a2r * a1i + a2i * a1r,
            a2r * b1r - a2i * b1i + b2r,
            a2r * b1i + a2i * b1r + b2i)


def s5_scan(u, h0_re, h0_im, ab_re, ab_im, bb_re, bb_im, c_re, c_im):
    bsz, seq_len, _ = u.shape
    chunk = SSM_CHUNK if seq_len % SSM_CHUNK == 0 else seq_len
    n_chunks = seq_len // chunk
    ug = u.astype(jnp.float32).reshape(bsz, n_chunks, chunk, SSM_GROUPS, SSM_GROUP).transpose(1, 0, 2, 3, 4)
    a_re = jnp.broadcast_to(ab_re, (bsz, chunk) + ab_re.shape)
    a_im = jnp.broadcast_to(ab_im, (bsz, chunk) + ab_im.shape)

    def step(carry, u_blk):
        hr, hi = carry
        bu_r = jnp.einsum('bcgi,gpi->bcgp', u_blk, bb_re)
        bu_i = jnp.einsum('bcgi,gpi->bcgp', u_blk, bb_im)
        bu_r = bu_r.at[:, 0].add(ab_re * hr - ab_im * hi)
        bu_i = bu_i.at[:, 0].add(ab_re * hi + ab_im * hr)
        _, _, sr, si = lax.associative_scan(_complex_affine_combine, (a_re, a_im, bu_r, bu_i), axis=1)
        y = jnp.einsum('bcgp,gip->bcgi', sr, c_re) - jnp.einsum('bcgp,gip->bcgi', si, c_im)
        return (sr[:, -1], si[:, -1]), y

    (hr, hi), ys = lax.scan(step, (h0_re.astype(jnp.float32), h0_im.astype(jnp.float32)), ug)
    y = ys.transpose(1, 0, 2, 3, 4).reshape(bsz, seq_len, SSM_WIDTH)
    return y, hr, hi


def ssm_layer(x, h0_re, h0_im, norm_pre, w_in, lam_re, lam_im, log_dt, b_re, b_im,
              c_re, c_im, d_skip, w_glu, b_glu, w_out, norm_post):
    h = rms_norm(x, norm_pre)
    u, z = jnp.split(h @ w_in, 2, axis=-1)
    ab_re, ab_im, bb_re, bb_im = s5_discretize(lam_re, lam_im, log_dt, b_re, b_im)
    y, hr, hi = s5_scan(u, h0_re, h0_im, ab_re, ab_im, bb_re, bb_im,
                        c_re.astype(jnp.float32), c_im.astype(jnp.float32))
    y = (y + d_skip.astype(jnp.float32) * u.astype(jnp.float32)).astype(x.dtype)
    ga, gb = jnp.split(jax.nn.gelu(y) @ w_glu + b_glu, 2, axis=-1)
    y = ga * jax.nn.sigmoid(gb) * jax.nn.silu(z)
    out = y @ w_out
    return x + rms_norm(out, norm_post), hr, hi


def shared_kv(x, kv_norm, w_kv):
    bsz, seq_len, _ = x.shape
    kv = rms_norm(x, kv_norm) @ w_kv
    return kv.reshape(bsz, seq_len, 2, N_HEADS, HEAD_DIM)


def dilated_prompt_attn(q, slopes, n_strides, dilation, k, v):
    bsz, seq_len, nh, dh = q.shape
    n = seq_len // dilation
    qb_len = n_strides
    nb = -(-n // qb_len)
    n_pad = nb * qb_len
    rb = bsz * dilation

    def to_res(a):
        return a.reshape(bsz, n, dilation, nh, dh).transpose(0, 2, 1, 3, 4).reshape(rb, n, nh, dh).astype(jnp.float32)

    qr = jnp.pad(to_res(q), ((0, 0), (0, n_pad - n), (0, 0), (0, 0)))
    kr = jnp.pad(to_res(k), ((0, 0), (qb_len, n_pad - n), (0, 0), (0, 0)))
    vr = jnp.pad(to_res(v), ((0, 0), (qb_len, n_pad - n), (0, 0), (0, 0)))
    qb = qr.reshape(rb, nb, qb_len, nh, dh)

    def band(a):
        return jnp.concatenate([a[:, :n_pad].reshape(rb, nb, qb_len, nh, dh),
                                a[:, qb_len:].reshape(rb, nb, qb_len, nh, dh)], axis=2)

    kb, vb = band(kr), band(vr)
    scores = jnp.einsum('nbqhd,nbkhd->nbhqk', qb, kb) * (1.0 / math.sqrt(dh))
    qi = jnp.arange(qb_len)[:, None]
    kj = jnp.arange(2 * qb_len)[None, :]
    dist = qi + qb_len - kj
    key_idx = jnp.arange(nb)[:, None, None] * qb_len - qb_len + kj[None]
    mask = (dist >= 0) & (dist <= n_strides) & (key_idx >= 0)
    bias = -slopes.astype(jnp.float32)[:, None, None] * (dist * dilation).astype(jnp.float32)
    scores = jnp.where(mask[None, :, None], scores + bias[None, None], NEG_INF)
    lse = jax.nn.logsumexp(scores, axis=-1)
    p = jnp.exp(scores - lse[..., None])
    o = jnp.einsum('nbhqk,nbkhd->nbqhd', p, vb).reshape(rb, n_pad, nh, dh)[:, :n]
    o = o.reshape(bsz, dilation, n, nh, dh).transpose(0, 2, 1, 3, 4).reshape(bsz, seq_len, nh, dh)
    lse = lse.transpose(0, 1, 3, 2).reshape(rb, n_pad, nh)[:, :n]
    lse = lse.reshape(bsz, dilation, n, nh).transpose(0, 2, 1, 3).reshape(bsz, seq_len, nh)
    return o, lse


def dilated_sample_attn(q, slopes, n_strides, dilation, k_all, v_all, n_buf):
    _, t_new, _, dh = q.shape
    j = jnp.arange(n_strides + 1)
    rows = n_buf + jnp.arange(t_new)[:, None] - j[None, :] * dilation
    valid = rows >= 0
    rows_c = jnp.maximum(rows, 0)
    kg = k_all[:, rows_c].astype(jnp.float32)
    vg = v_all[:, rows_c].astype(jnp.float32)
    scores = jnp.einsum('bthd,btjhd->bhtj', q.astype(jnp.float32), kg) * (1.0 / math.sqrt(dh))
    scores = scores - slopes.astype(jnp.float32)[:, None, None] * (j * dilation).astype(jnp.float32)[None, None, :]
    scores = jnp.where(valid[None, None], scores, NEG_INF)
    lse = jax.nn.logsumexp(scores, axis=-1)
    p = jnp.exp(scores - lse[..., None])
    o = jnp.einsum('bhtj,btjhd->bthd', p, vg)
    return o, lse.transpose(0, 2, 1)


def dilated_mixer_layer(x, attn_fn, norm_pre, w_in, w_out, norm_post):
    bsz, seq_len, _ = x.shape
    h = rms_norm(x, norm_pre)
    proj = h @ w_in
    q = proj[..., :N_GROUPS * ATTN_WIDTH].reshape(bsz, seq_len, N_GROUPS, N_HEADS, HEAD_DIM)
    z = proj[..., N_GROUPS * ATTN_WIDTH:]
    slopes = alibi_slopes()
    outs, lses = [], []
    for g, (window, dilation) in enumerate(DILATION_GROUPS):
        o, lse = attn_fn(q[:, :, g], slopes[g], window // dilation, dilation)
        outs.append(o)
        lses.append(lse)
    wgt = jax.nn.softmax(jnp.stack(lses), axis=0)
    o = jnp.einsum('gblh,gblhd->blhd', wgt, jnp.stack(outs))
    o = o.reshape(bsz, seq_len, ATTN_WIDTH).astype(x.dtype) * jax.nn.silu(z)
    return x + rms_norm(o @ w_out, norm_post)


def trunk(x, h0_re, h0_im, kv_past, a_norm_pre, a_w_in, a_lam_re, a_lam_im, a_log_dt,
          a_b_re, a_b_im, a_c_re, a_c_im, a_d, a_w_glu, a_b_glu, a_w_out, a_norm_post,
          kv_norm, w_kv, b_norm_pre, b_w_in, b_w_out, b_norm_post):
    new_re, new_im = [], []
    attn_fn, kv_state = None, None
    for layer in range(DEPTH):
        if layer < N_A_LAYERS:
            x, hr, hi = ssm_layer(x, h0_re[layer], h0_im[layer], a_norm_pre[layer], a_w_in[layer],
                                  a_lam_re[layer], a_lam_im[layer], a_log_dt[layer], a_b_re[layer],
                                  a_b_im[layer], a_c_re[layer], a_c_im[layer], a_d[layer],
                                  a_w_glu[layer], a_b_glu[layer], a_w_out[layer], a_norm_post[layer])
            new_re.append(hr.astype(h0_re.dtype))
            new_im.append(hi.astype(h0_im.dtype))
            continue
        if layer == N_A_LAYERS:
            kv = shared_kv(x, kv_norm, w_kv)
            if kv_past is None:
                seq_len = x.shape[1]
                n_keep = min(MAX_WINDOW, seq_len)
                attn_fn = functools.partial(dilated_prompt_attn, k=kv[:, :, 0], v=kv[:, :, 1])
                kv_state = kv[:, seq_len - n_keep:]
            else:
                n_buf = kv_past.shape[1]
                kv_all = jnp.concatenate([kv_past.astype(kv.dtype), kv], axis=1)
                attn_fn = functools.partial(dilated_sample_attn, k_all=kv_all[:, :, 0],
                                            v_all=kv_all[:, :, 1], n_buf=n_buf)
                kv_state = kv_all[:, kv_all.shape[1] - n_buf:]
        lb = layer - N_A_LAYERS
        x = dilated_mixer_layer(x, attn_fn, b_norm_pre[lb], b_w_in[lb], b_w_out[lb], b_norm_post[lb])
    return x, jnp.stack(new_re), jnp.stack(new_im), kv_state


def setup_inputs(seed: int = 0) -> dict:
    key = jax.random.key(seed)
    ks = jax.random.split(key, 32)
    f32 = jnp.float32
    n_buf = min(MAX_WINDOW, PAST_LEN)
    nrm = lambda k, shape, s: jax.random.normal(k, shape, f32) * s
    lam_im = math.pi * jnp.arange(SSM_STATE, dtype=f32)
    return {
        'x_prompt': nrm(ks[0], (BATCH, SEQ, D_MODEL), 1.0),
        'x_sample': nrm(ks[1], (DEC_BATCH, DEC_SEQ, D_MODEL), 1.0),
        'state_s5_re': nrm(ks[2], (N_A_LAYERS, DEC_BATCH, SSM_GROUPS, SSM_STATE), 0.5),
        'state_s5_im': nrm(ks[3], (N_A_LAYERS, DEC_BATCH, SSM_GROUPS, SSM_STATE), 0.5),
        'cache_kv': nrm(ks[4], (DEC_BATCH, n_buf, 2, N_HEADS, HEAD_DIM), 1.0),
        'a_norm_pre': 1.0 + nrm(ks[5], (N_A_LAYERS, D_MODEL), 0.05),
        'a_w_in': nrm(ks[6], (N_A_LAYERS, D_MODEL, 2 * SSM_WIDTH), D_MODEL ** -0.5),
        'a_lam_re': -0.5 + nrm(ks[7], (N_A_LAYERS, SSM_GROUPS, SSM_STATE), 0.01),
        'a_lam_im': lam_im + nrm(ks[8], (N_A_LAYERS, SSM_GROUPS, SSM_STATE), 0.01),
        'a_log_dt': jax.random.uniform(ks[9], (N_A_LAYERS, SSM_GROUPS), f32, math.log(DT_MIN), math.log(DT_MAX)),
        'a_b_re': nrm(ks[10], (N_A_LAYERS, SSM_GROUPS, SSM_STATE, SSM_GROUP), (2 * SSM_GROUP) ** -0.5),
        'a_b_im': nrm(ks[11], (N_A_LAYERS, SSM_GROUPS, SSM_STATE, SSM_GROUP), (2 * SSM_GROUP) ** -0.5),
        'a_c_re': nrm(ks[12], (N_A_LAYERS, SSM_GROUPS, SSM_GROUP, SSM_STATE), SSM_STATE ** -0.5),
        'a_c_im': nrm(ks[13], (N_A_LAYERS, SSM_GROUPS, SSM_GROUP, SSM_STATE), SSM_STATE ** -0.5),
        'a_d': nrm(ks[14], (N_A_LAYERS, SSM_WIDTH), 1.0),
        'a_w_glu': nrm(ks[15], (N_A_LAYERS, SSM_WIDTH, 2 * SSM_WIDTH), SSM_WIDTH ** -0.5),
        'a_b_glu': nrm(ks[16], (N_A_LAYERS, 2 * SSM_WIDTH), 0.01),
        'a_w_out': nrm(ks[17], (N_A_LAYERS, SSM_WIDTH, D_MODEL), SSM_WIDTH ** -0.5),
        'a_norm_post': 1.0 + nrm(ks[18], (N_A_LAYERS, D_MODEL), 0.05),
        'kv_norm': 1.0 + nrm(ks[19], (D_MODEL,), 0.05),
        'w_kv': nrm(ks[20], (D_MODEL, 2 * ATTN_WIDTH), D_MODEL ** -0.5),
        'b_norm_pre': 1.0 + nrm(ks[21], (N_B_LAYERS, D_MODEL), 0.05),
        'b_w_in': nrm(ks[22], (N_B_LAYERS, D_MODEL, (N_GROUPS + 1) * ATTN_WIDTH), D_MODEL ** -0.5),
        'b_w_out': nrm(ks[23], (N_B_LAYERS, ATTN_WIDTH, D_MODEL), ATTN_WIDTH ** -0.5),
        'b_norm_post': 1.0 + nrm(ks[24], (N_B_LAYERS, D_MODEL), 0.05),
    }


def reference(x_prompt, x_sample, state_s5_re, state_s5_im, cache_kv, a_norm_pre, a_w_in,
              a_lam_re, a_lam_im, a_log_dt, a_b_re, a_b_im, a_c_re, a_c_im, a_d, a_w_glu,
              a_b_glu, a_w_out, a_norm_post, kv_norm, w_kv, b_norm_pre, b_w_in, b_w_out,
              b_norm_post):
    h0 = jnp.zeros((N_A_LAYERS, x_prompt.shape[0], SSM_GROUPS, SSM_STATE), x_prompt.dtype)
    y_prompt, p_re, p_im, p_kv = trunk(
        x_prompt, h0, h0, None, a_norm_pre, a_w_in, a_lam_re, a_lam_im, a_log_dt,
        a_b_re, a_b_im, a_c_re, a_c_im, a_d, a_w_glu, a_b_glu, a_w_out, a_norm_post,
        kv_norm, w_kv, b_norm_pre, b_w_in, b_w_out, b_norm_post)
    y_sample, s_re, s_im, s_kv = trunk(
        x_sample, state_s5_re, state_s5_im, cache_kv, a_norm_pre, a_w_in, a_lam_re, a_lam_im,
        a_log_dt, a_b_re, a_b_im, a_c_re, a_c_im, a_d, a_w_glu, a_b_glu, a_w_out, a_norm_post,
        kv_norm, w_kv, b_norm_pre, b_w_in, b_w_out, b_norm_post)
    return (y_prompt, y_sample, p_re, p_im, p_kv, s_re, s_im, s_kv)
```

```python
import functools
import math

import jax
import jax.numpy as jnp
from jax import lax
from jax.experimental import pallas as pl
from jax.experimental.pallas import tpu as pltpu

D_MODEL = 1024
SSM_WIDTH = 2048
SSM_GROUP = 16
SSM_GROUPS = SSM_WIDTH // SSM_GROUP
SSM_STATE = 64
HEAD_DIM = 64
N_HEADS = 16
ATTN_WIDTH = N_HEADS * HEAD_DIM
DILATIONS = (1, 4, 16)
N_STRIDES = 128
N_GROUPS = len(DILATIONS)
ALIBI_MAX_EXP = 8.0
RMS_EPS = 1e-6
NEG_INF = -1e30

VMEM_LIMIT_BYTES_V7X = 56 * 1024 * 1024
LANES = 128

GROUPS_PER_BLOCK = 16
N_GROUP_BLOCKS = SSM_GROUPS // GROUPS_PER_BLOCK
BLOCK_CHANNELS = GROUPS_PER_BLOCK * SSM_GROUP
BLOCK_STATES = GROUPS_PER_BLOCK * SSM_STATE


def _params(*semantics):
    return pltpu.CompilerParams(dimension_semantics=semantics,
                                vmem_limit_bytes=VMEM_LIMIT_BYTES_V7X)


def _rms_scale(x):
    return lax.rsqrt(jnp.mean(x * x, axis=-1, keepdims=True) + RMS_EPS)


def _norm_matmul_kernel(x_ref, g_ref, w_ref, o_ref, xn_ref):
    @pl.when(pl.program_id(2) == 0)
    def _():
        x = x_ref[...]
        xn_ref[...] = (x * _rms_scale(x) * g_ref[...]).astype(jnp.bfloat16)

    o_ref[...] = jnp.dot(xn_ref[...], w_ref[...], preferred_element_type=jnp.float32)


def norm_matmul(x, g, w, *, time_major_out, tm=512, tn=1024):
    bsz, seq, d = x.shape
    n = w.shape[1]
    tm = min(tm, seq)
    tn = min(tn, n)
    nj = n // tn
    if time_major_out:
        out_shape = jax.ShapeDtypeStruct((seq, bsz * n), jnp.float32)
        out_spec = pl.BlockSpec((tm, tn), lambda b, i, j: (i, b * nj + j))
    else:
        out_shape = jax.ShapeDtypeStruct((bsz, seq, n), jnp.float32)
        out_spec = pl.BlockSpec((None, tm, tn), lambda b, i, j: (b, i, j))
    return pl.pallas_call(
        _norm_matmul_kernel,
        grid=(bsz, seq // tm, nj),
        in_specs=[
            pl.BlockSpec((None, tm, d), lambda b, i, j: (b, i, 0)),
            pl.BlockSpec((1, d), lambda b, i, j: (0, 0)),
            pl.BlockSpec((d, tn), lambda b, i, j: (0, j)),
        ],
        out_specs=out_spec,
        out_shape=out_shape,
        scratch_shapes=[pltpu.VMEM((tm, d), jnp.bfloat16)],
        compiler_params=_params("parallel", "parallel", "arbitrary"),
    )(x, g.reshape(1, d), w)


def _kv_kernel(x_ref, g_ref, w_ref, wt_ref, kv_ref, kvt_ref):
    x = x_ref[...]
    xn = (x * _rms_scale(x) * g_ref[...]).astype(jnp.bfloat16)
    kv_ref[...] = jnp.dot(xn, w_ref[...], preferred_element_type=jnp.float32)
    kvt_ref[...] = lax.dot_general(wt_ref[...], xn, (((1,), (1,)), ((), ())),
                                   preferred_element_type=jnp.float32)


def kv_project(x, g, w, wt, *, tm=256):
    bsz, seq, d = x.shape
    n = w.shape[1]
    tm = min(tm, seq)
    return pl.pallas_call(
        _kv_kernel,
        grid=(bsz, seq // tm),
        in_specs=[
            pl.BlockSpec((None, tm, d), lambda b, i: (b, i, 0)),
            pl.BlockSpec((1, d), lambda b, i: (0, 0)),
            pl.BlockSpec((d, n), lambda b, i: (0, 0)),
            pl.BlockSpec((n, d), lambda b, i: (0, 0)),
        ],
        out_specs=[
            pl.BlockSpec((None, tm, n), lambda b, i: (b, i, 0)),
            pl.BlockSpec((None, n, tm), lambda b, i: (b, 0, i)),
        ],
        out_shape=[
            jax.ShapeDtypeStruct((bsz, seq, n), jnp.float32),
            jax.ShapeDtypeStruct((bsz, n, seq), jnp.float32),
        ],
        compiler_params=_params("parallel", "parallel"),
    )(x, g.reshape(1, d), w, wt)


def _discretize_kernel(lam_re_ref, lam_im_ref, log_dt_ref, bt_re_ref, bt_im_ref,
                       ab_re_ref, ab_im_ref, bbt_re_ref, bbt_im_ref):
    lam_re = lam_re_ref[...]
    lam_im = lam_im_ref[...]
    dt = jnp.exp(log_dt_ref[...])
    mag = jnp.exp(lam_re * dt)
    ab_re = mag * jnp.cos(lam_im * dt)
    ab_im = mag * jnp.sin(lam_im * dt)
    den = lam_re * lam_re + lam_im * lam_im
    num_re = ab_re - 1.0
    coef_re = (num_re * lam_re + ab_im * lam_im) / den
    coef_im = (ab_im * lam_re - num_re * lam_im) / den
    ab_re_ref[...] = ab_re
    ab_im_ref[...] = ab_im
    b_re = bt_re_ref[...]
    b_im = bt_im_ref[...]
    cr = coef_re[:, None, :]
    ci = coef_im[:, None, :]
    bbt_re_ref[...] = cr * b_re - ci * b_im
    bbt_im_ref[...] = cr * b_im + ci * b_re


def discretize(lam_re, lam_im, log_dt, b_re, b_im):
    g, p = lam_re.shape
    gc = b_re.shape[-1]
    bt_re = jnp.swapaxes(b_re, 1, 2)
    bt_im = jnp.swapaxes(b_im, 1, 2)
    f32 = jnp.float32
    return pl.pallas_call(
        _discretize_kernel,
        out_shape=[jax.ShapeDtypeStruct((g, p), f32), jax.ShapeDtypeStruct((g, p), f32),
                   jax.ShapeDtypeStruct((g, gc, p), f32), jax.ShapeDtypeStruct((g, gc, p), f32)],
    )(lam_re, lam_im, log_dt.reshape(g, 1), bt_re, bt_im)


def _block_diag(w, rows_first):
    gc, p = w.shape[1], w.shape[2]
    w4 = w.reshape(N_GROUP_BLOCKS, GROUPS_PER_BLOCK, gc, p)
    eye = jnp.eye(GROUPS_PER_BLOCK, dtype=bool)
    if rows_first:
        bd = jnp.where(eye[None, :, None, :, None], w4[:, :, :, None, :], 0.0)
        return bd.reshape(N_GROUP_BLOCKS, GROUPS_PER_BLOCK * gc, GROUPS_PER_BLOCK * p)
    wt = jnp.swapaxes(w4, 2, 3)
    bd = jnp.where(eye[None, :, None, :, None], wt[:, :, :, None, :], 0.0)
    return bd.reshape(N_GROUP_BLOCKS, GROUPS_PER_BLOCK * p, GROUPS_PER_BLOCK * gc)


def _ssm_kernel(u_ref, bre_ref, bim_ref, cre_ref, cim_ref, ar_ref, ai_ref, d_ref,
                h0re_ref, h0im_ref, g_ref, hre_out_ref, him_out_ref,
                sre_ref, sim_ref, hre_ref, him_ref, *, n_steps, bsz):
    tc = pl.program_id(1)

    @pl.when(tc == 0)
    def _():
        hre_ref[...] = h0re_ref[...]
        him_ref[...] = h0im_ref[...]

    u = u_ref[...]
    ub = u.astype(jnp.bfloat16)
    sre_ref[...] = jnp.dot(ub, bre_ref[...], preferred_element_type=jnp.float32)
    sim_ref[...] = jnp.dot(ub, bim_ref[...], preferred_element_type=jnp.float32)

    ar = jnp.broadcast_to(ar_ref[...], (bsz, BLOCK_STATES))
    ai = jnp.broadcast_to(ai_ref[...], (bsz, BLOCK_STATES))

    def step(t, carry):
        hr, hi = carry
        rows = pl.ds(pl.multiple_of(t * bsz, bsz), bsz)
        nhr = ar * hr - ai * hi + sre_ref[rows, :]
        nhi = ar * hi + ai * hr + sim_ref[rows, :]
        sre_ref[rows, :] = nhr
        sim_ref[rows, :] = nhi
        return nhr, nhi

    carry = (hre_ref[...], him_ref[...])
    if n_steps == 1:
        hr, hi = step(0, carry)
    else:
        hr, hi = lax.fori_loop(0, n_steps, step, carry, unroll=2)
    hre_ref[...] = hr
    him_ref[...] = hi

    y = jnp.dot(sre_ref[...].astype(jnp.bfloat16), cre_ref[...], preferred_element_type=jnp.float32)
    y = y - jnp.dot(sim_ref[...].astype(jnp.bfloat16), cim_ref[...], preferred_element_type=jnp.float32)
    y = y + d_ref[...] * u
    g_ref[...] = jax.nn.gelu(y).astype(jnp.bfloat16)

    @pl.when(tc == pl.num_programs(1) - 1)
    def _():
        hre_out_ref[...] = hr
        him_out_ref[...] = hi


def ssm_core(uz, bsz, bre, bim, cre, cim, ar, ai, d_skip, h0_re, h0_im, *, n_steps):
    rows = uz.shape[0]
    seq = rows // bsz
    n_steps = min(n_steps, seq)
    tr = n_steps * bsz
    kern = functools.partial(_ssm_kernel, n_steps=n_steps, bsz=bsz)
    f32 = jnp.float32
    wspec = lambda shape: pl.BlockSpec((None,) + shape, lambda gb, tc: (gb, 0, 0))
    return pl.pallas_call(
        kern,
        grid=(N_GROUP_BLOCKS, seq // n_steps),
        in_specs=[
            pl.BlockSpec((tr, BLOCK_CHANNELS), lambda gb, tc: (tc, gb)),
            wspec((BLOCK_CHANNELS, BLOCK_STATES)), wspec((BLOCK_CHANNELS, BLOCK_STATES)),
            wspec((BLOCK_STATES, BLOCK_CHANNELS)), wspec((BLOCK_STATES, BLOCK_CHANNELS)),
            wspec((1, BLOCK_STATES)), wspec((1, BLOCK_STATES)), wspec((1, BLOCK_CHANNELS)),
            pl.BlockSpec((bsz, BLOCK_STATES), lambda gb, tc: (0, gb)),
            pl.BlockSpec((bsz, BLOCK_STATES), lambda gb, tc: (0, gb)),
        ],
        out_specs=[
            pl.BlockSpec((tr, BLOCK_CHANNELS), lambda gb, tc: (tc, gb)),
            pl.BlockSpec((bsz, BLOCK_STATES), lambda gb, tc: (0, gb)),
            pl.BlockSpec((bsz, BLOCK_STATES), lambda gb, tc: (0, gb)),
        ],
        out_shape=[
            jax.ShapeDtypeStruct((rows, SSM_WIDTH), jnp.bfloat16),
            jax.ShapeDtypeStruct((bsz, SSM_GROUPS * SSM_STATE), f32),
            jax.ShapeDtypeStruct((bsz, SSM_GROUPS * SSM_STATE), f32),
        ],
        scratch_shapes=[pltpu.VMEM((tr, BLOCK_STATES), f32), pltpu.VMEM((tr, BLOCK_STATES), f32),
                        pltpu.VMEM((bsz, BLOCK_STATES), f32), pltpu.VMEM((bsz, BLOCK_STATES), f32)],
        compiler_params=_params("parallel", "arbitrary"),
    )(uz, bre, bim, cre, cim, ar, ai, d_skip, h0_re, h0_im)


def _glu_kernel(g_ref, wa_ref, wb_ref, ba_ref, bb_ref, z_ref, o_ref):
    g = g_ref[...]
    a = jnp.dot(g, wa_ref[...], preferred_element_type=jnp.float32) + ba_ref[...]
    b = jnp.dot(g, wb_ref[...], preferred_element_type=jnp.float32) + bb_ref[...]
    o_ref[...] = (a * jax.nn.sigmoid(b) * jax.nn.silu(z_ref[...])).astype(jnp.bfloat16)


def glu(g, w_glu, b_glu, uz, *, tm=512, tn=512):
    rows, e = g.shape
    tm = min(tm, rows)
    nj = e // tn
    return pl.pallas_call(
        _glu_kernel,
        grid=(nj, rows // tm),
        in_specs=[
            pl.BlockSpec((tm, e), lambda j, i: (i, 0)),
            pl.BlockSpec((e, tn), lambda j, i: (0, j)),
            pl.BlockSpec((e, tn), lambda j, i: (0, nj + j)),
            pl.BlockSpec((1, tn), lambda j, i: (0, j)),
            pl.BlockSpec((1, tn), lambda j, i: (0, nj + j)),
            pl.BlockSpec((tm, tn), lambda j, i: (i, nj + j)),
        ],
        out_specs=pl.BlockSpec((tm, tn), lambda j, i: (i, j)),
        out_shape=jax.ShapeDtypeStruct((rows, e), jnp.bfloat16),
        compiler_params=_params("parallel", "parallel"),
    )(g, w_glu, w_glu, b_glu, b_glu, uz)


def _out_proj_kernel(h_ref, w_ref, g_ref, x_ref, o_ref):
    y = jnp.dot(h_ref[...], w_ref[...], preferred_element_type=jnp.float32)
    o_ref[...] = x_ref[...] + y * _rms_scale(y) * g_ref[...]


def _gated_out_proj_kernel(a_ref, z_ref, w_ref, g_ref, x_ref, o_ref):
    h = (a_ref[...] * jax.nn.silu(z_ref[...])).astype(jnp.bfloat16)
    y = jnp.dot(h, w_ref[...], preferred_element_type=jnp.float32)
    o_ref[...] = x_ref[...] + y * _rms_scale(y) * g_ref[...]


def out_proj(h, w, g, x, *, time_major_in, tm=512):
    bsz, seq, d = x.shape
    k = w.shape[0]
    tm = min(tm, seq)
    if time_major_in:
        h_spec = pl.BlockSpec((tm, k), lambda b, i: (i, b))
    else:
        h_spec = pl.BlockSpec((None, tm, k), lambda b, i: (b, i, 0))
    return pl.pallas_call(
        _out_proj_kernel,
        grid=(bsz, seq // tm),
        in_specs=[
            h_spec,
            pl.BlockSpec((k, d), lambda b, i: (0, 0)),
            pl.BlockSpec((1, d), lambda b, i: (0, 0)),
            pl.BlockSpec((None, tm, d), lambda b, i: (b, i, 0)),
        ],
        out_specs=pl.BlockSpec((None, tm, d), lambda b, i: (b, i, 0)),
        out_shape=jax.ShapeDtypeStruct((bsz, seq, d), jnp.float32),
        compiler_params=_params("parallel", "parallel"),
    )(h, w, g.reshape(1, d), x)


def gated_out_proj(a, proj, z_block, w, g, x):
    rows, d = x.shape
    k = w.shape[0]
    return pl.pallas_call(
        _gated_out_proj_kernel,
        grid=(1,),
        in_specs=[
            pl.BlockSpec((rows, k), lambda i: (0, 0)),
            pl.BlockSpec((rows, k), lambda i: (0, z_block)),
            pl.BlockSpec((k, d), lambda i: (0, 0)),
            pl.BlockSpec((1, d), lambda i: (0, 0)),
            pl.BlockSpec((rows, d), lambda i: (0, 0)),
        ],
        out_specs=pl.BlockSpec((rows, d), lambda i: (0, 0)),
        out_shape=jax.ShapeDtypeStruct((rows, d), jnp.float32),
        compiler_params=_params("arbitrary"),
    )(a, proj, w, g.reshape(1, d), x)


QB = N_STRIDES
HEAD_PAIR = 2 * HEAD_DIM


def _prompt_attn_kernel(slopes_ref, q0_ref, q1_ref, q2_ref, z_ref, k_ref, v_ref, o_ref,
                        acc0, acc1, acc2, m0, m1, m2, l0, l1, l2, *, seq):
    hp = pl.program_id(1)
    q_refs = (q0_ref, q1_ref, q2_ref)
    accs, ms, ls = (acc0, acc1, acc2), (m0, m1, m2), (l0, l1, l2)
    lane = lax.broadcasted_iota(jnp.int32, (QB, HEAD_PAIR), 1)
    first_head = lane < HEAD_DIM
    qi = lax.broadcasted_iota(jnp.int32, (QB, QB), 0)
    kj = lax.broadcasted_iota(jnp.int32, (QB, QB), 1)
    cur_mask = kj <= qi
    prev_mask = kj >= qi
    cur_dist = (qi - kj).astype(jnp.float32)
    prev_dist = (qi + QB - kj).astype(jnp.float32)
    nt = (((1,), (1,)), ((), ()))
    blocks_per_group = seq // QB

    for g, dil in enumerate(DILATIONS):
        n_res = seq // dil
        blocks_per_res = n_res // QB
        has_prev_block = blocks_per_res > 1

        def body(j, _, g=g, dil=dil, blocks_per_res=blocks_per_res, has_prev_block=has_prev_block):
            if dil == 1:
                start = pl.multiple_of(j * QB, QB)
                prev_start = jnp.maximum(start - QB, 0)
                rows = lambda s: pl.ds(s, QB)
                has_prev = j > 0
            else:
                blk = j // dil
                res = j - blk * dil
                start = res + blk * (QB * dil)
                prev_start = jnp.maximum(start - QB * dil, 0)
                rows = lambda s: pl.ds(s, QB, stride=dil)
                has_prev = blk > 0
            q2 = q_refs[g][rows(start), :] * (1.0 / math.sqrt(HEAD_DIM))
            kc = k_ref[rows(start), :].astype(jnp.bfloat16)
            vc = v_ref[rows(start), :].astype(jnp.bfloat16)
            if has_prev_block:
                kp = k_ref[rows(prev_start), :].astype(jnp.bfloat16)
                vp = v_ref[rows(prev_start), :].astype(jnp.bfloat16)
                pmask = jnp.logical_and(prev_mask, has_prev)
            outs, mxs, sms = [], [], []
            for hh in range(2):
                slope = slopes_ref[g * N_HEADS + 2 * hp + hh] * float(dil)
                qh = jnp.where(first_head if hh == 0 else jnp.logical_not(first_head), q2, 0.0)
                qh = qh.astype(jnp.bfloat16)
                sc = lax.dot_general(qh, kc, nt, preferred_element_type=jnp.float32)
                sc = jnp.where(cur_mask, sc - slope * cur_dist, NEG_INF)
                mx = jnp.max(sc, axis=-1, keepdims=True)
                if has_prev_block:
                    sp = lax.dot_general(qh, kp, nt, preferred_element_type=jnp.float32)
                    sp = jnp.where(pmask, sp - slope * prev_dist, NEG_INF)
                    mx = jnp.maximum(mx, jnp.max(sp, axis=-1, keepdims=True))
                pc = jnp.exp(sc - mx)
                sm = jnp.sum(pc, axis=-1, keepdims=True)
                o = jnp.dot(pc.astype(jnp.bfloat16), vc, preferred_element_type=jnp.float32)
                if has_prev_block:
                    pp = jnp.exp(sp - mx)
                    sm = sm + jnp.sum(pp, axis=-1, keepdims=True)
                    o = o + jnp.dot(pp.astype(jnp.bfloat16), vp, preferred_element_type=jnp.float32)
                outs.append(o)
                mxs.append(jnp.broadcast_to(mx, (QB, HEAD_PAIR)))
                sms.append(jnp.broadcast_to(sm, (QB, HEAD_PAIR)))
            accs[g][rows(start), :] = jnp.where(first_head, outs[0], outs[1])
            ms[g][rows(start), :] = jnp.where(first_head, mxs[0], mxs[1])
            ls[g][rows(start), :] = jnp.where(first_head, sms[0], sms[1])
            return 0

        lax.fori_loop(0, blocks_per_group, body, 0)

    def merge(i, _):
        rows = pl.ds(pl.multiple_of(i * QB, QB), QB)
        mm = jnp.maximum(jnp.maximum(m0[rows, :], m1[rows, :]), m2[rows, :])
        num = jnp.zeros((QB, HEAD_PAIR), jnp.float32)
        den = jnp.zeros((QB, HEAD_PAIR), jnp.float32)
        for g in range(N_GROUPS):
            e = jnp.exp(ms[g][rows, :] - mm)
            num = num + e * accs[g][rows, :]
            den = den + e * ls[g][rows, :]
        o_ref[rows, :] = ((num / den) * jax.nn.silu(z_ref[rows, :])).astype(jnp.bfloat16)
        return 0

    lax.fori_loop(0, seq // QB, merge, 0)


def prompt_attention(slopes, proj, kv):
    bsz, seq, _ = proj.shape
    n_pairs = ATTN_WIDTH // HEAD_PAIR
    col = lambda off: pl.BlockSpec((None, seq, HEAD_PAIR), lambda b, hp, off=off: (b, 0, off + hp))
    f32 = jnp.float32
    return pl.pallas_call(
        functools.partial(_prompt_attn_kernel, seq=seq),
        grid=(bsz, n_pairs),
        in_specs=[pl.BlockSpec(memory_space=pltpu.SMEM),
                  col(0), col(n_pairs), col(2 * n_pairs), col(3 * n_pairs), col(0), col(n_pairs)],
        out_specs=pl.BlockSpec((None, seq, HEAD_PAIR), lambda b, hp: (b, 0, hp)),
        out_shape=jax.ShapeDtypeStruct((bsz, seq, ATTN_WIDTH), jnp.bfloat16),
        scratch_shapes=[pltpu.VMEM((seq, HEAD_PAIR), f32) for _ in range(9)],
        compiler_params=_params("parallel", "parallel"),
    )(slopes, proj, proj, proj, proj, kv, kv)


def _cache_shift_kernel(c_ref, new_ref, o_ref):
    b = pl.program_id(0)
    rows, n_pos = c_ref.shape
    o_ref[...] = pltpu.roll(c_ref[...], n_pos - 1, 1)
    new = new_ref[...]
    lane = lax.broadcasted_iota(jnp.int32, new.shape, 1)
    new_col = jnp.sum(jnp.where(lane == b, new, 0.0), axis=1, keepdims=True)
    last = o_ref[:, n_pos - LANES:]
    last_lane = lax.broadcasted_iota(jnp.int32, last.shape, 1) == LANES - 1
    o_ref[:, n_pos - LANES:] = jnp.where(last_lane, new_col, last)


def cache_shift(cache_t, new_t, *, rows=256):
    bsz, chans, n_pos = cache_t.shape
    return pl.pallas_call(
        _cache_shift_kernel,
        grid=(bsz, chans // rows),
        in_specs=[pl.BlockSpec((None, rows, n_pos), lambda b, i: (b, i, 0)),
                  pl.BlockSpec((rows, bsz), lambda b, i: (i, 0))],
        out_specs=pl.BlockSpec((None, rows, n_pos), lambda b, i: (b, i, 0)),
        out_shape=jax.ShapeDtypeStruct(cache_t.shape, cache_t.dtype),
        compiler_params=_params("parallel", "parallel"),
    )(cache_t, new_t)


HEADS_PER_STEP = 4


def _sample_attn_kernel(slopes_ref, cols_ref, k_ref, v_ref, o_ref, *, n_pos):
    hb = pl.program_id(1)
    pos = lax.broadcasted_iota(jnp.int32, (1, n_pos), 1)
    dist = (n_pos - pos).astype(jnp.float32)
    n_q = N_GROUPS * HEADS_PER_STEP
    for hh in range(HEADS_PER_STEP):
        kh = k_ref[hh * HEAD_DIM:(hh + 1) * HEAD_DIM, :]
        vh = v_ref[hh * HEAD_DIM:(hh + 1) * HEAD_DIM, :]
        k_new = cols_ref[:, n_q + hh:n_q + hh + 1]
        v_new = cols_ref[:, n_q + HEADS_PER_STEP + hh:n_q + HEADS_PER_STEP + hh + 1]
        scores, new_scores = [], []
        for g, dil in enumerate(DILATIONS):
            lo = n_pos - N_STRIDES * dil
            q = cols_ref[:, g * HEADS_PER_STEP + hh:g * HEADS_PER_STEP + hh + 1] * (1.0 / math.sqrt(HEAD_DIM))
            slope = slopes_ref[g * N_HEADS + hb * HEADS_PER_STEP + hh]
            s = jnp.sum(kh[:, lo:] * q, axis=0, keepdims=True) - slope * dist[:, lo:]
            if dil > 1:
                s = jnp.where((pos[:, lo:] & (dil - 1)) == 0, s, NEG_INF)
            scores.append(s)
            new_scores.append(jnp.sum(k_new * q, axis=0, keepdims=True))
        mx = new_scores[0]
        for g in range(N_GROUPS):
            mx = jnp.maximum(mx, jnp.maximum(new_scores[g], jnp.max(scores[g], axis=1, keepdims=True)))
        p_new = sum(jnp.exp(s - mx) for s in new_scores)
        den = p_new
        p_all = None
        for g in range(N_GROUPS - 1, -1, -1):
            p = jnp.exp(scores[g] - mx)
            den = den + jnp.sum(p, axis=1, keepdims=True)
            if p_all is None:
                p_all = p
            else:
                w = p.shape[1]
                p_all = jnp.concatenate([p_all[:, :n_pos - w], p_all[:, n_pos - w:] + p], axis=1)
        o = jnp.sum(vh * p_all, axis=1, keepdims=True) + p_new * v_new
        o_ref[:, hh:hh + 1] = o / den


def sample_attention(slopes, cols, cache_t):
    bsz, _, n_pos = cache_t.shape
    n_hb = N_HEADS // HEADS_PER_STEP
    rows = HEADS_PER_STEP * HEAD_DIM
    ncol = cols.shape[-1]
    return pl.pallas_call(
        functools.partial(_sample_attn_kernel, n_pos=n_pos),
        grid=(bsz, n_hb),
        in_specs=[pl.BlockSpec(memory_space=pltpu.SMEM),
                  pl.BlockSpec((None, None, HEAD_DIM, ncol), lambda b, h: (b, h, 0, 0)),
                  pl.BlockSpec((None, rows, n_pos), lambda b, h: (b, h, 0)),
                  pl.BlockSpec((None, rows, n_pos), lambda b, h: (b, n_hb + h, 0))],
        out_specs=pl.BlockSpec((None, None, HEAD_DIM, HEADS_PER_STEP), lambda b, h: (b, h, 0, 0)),
        out_shape=jax.ShapeDtypeStruct((bsz, n_hb, HEAD_DIM, HEADS_PER_STEP), jnp.float32),
        compiler_params=_params("parallel", "parallel"),
    )(slopes, cols, cache_t, cache_t)


def _alibi_slopes():
    n = N_GROUPS * N_HEADS
    e = jnp.arange(1, n + 1, dtype=jnp.float32)
    return jnp.exp2(-ALIBI_MAX_EXP * e / n)


def _ssm_weights(lam_re, lam_im, log_dt, b_re, b_im, c_re, c_im, d_skip):
    ab_re, ab_im, bbt_re, bbt_im = discretize(lam_re, lam_im, log_dt, b_re, b_im)
    bf16 = jnp.bfloat16
    return dict(
        bre=_block_diag(bbt_re, True).astype(bf16), bim=_block_diag(bbt_im, True).astype(bf16),
        cre=_block_diag(c_re, False).astype(bf16), cim=_block_diag(c_im, False).astype(bf16),
        ar=ab_re.reshape(N_GROUP_BLOCKS, 1, BLOCK_STATES), ai=ab_im.reshape(N_GROUP_BLOCKS, 1, BLOCK_STATES),
        d_skip=d_skip.reshape(N_GROUP_BLOCKS, 1, BLOCK_CHANNELS))


def _ssm_layer(x, h0_re, h0_im, norm_pre, w_in, ssm_w, w_glu, b_glu, w_out, norm_post, *, n_steps):
    bsz, seq, _ = x.shape
    uz = norm_matmul(x, norm_pre, w_in, time_major_out=True)
    uz = uz.reshape(seq * bsz, 2 * SSM_WIDTH)
    g, h_re, h_im = ssm_core(uz, h0_re.shape[0], h0_re=h0_re, h0_im=h0_im, n_steps=n_steps, **ssm_w)
    h2 = glu(g, w_glu, b_glu.reshape(1, -1), uz)
    x = out_proj(h2.reshape(seq, bsz * SSM_WIDTH), w_out, norm_post, x, time_major_in=True)
    return x, h_re, h_im


def _states_out(h, bsz):
    return h.reshape(bsz, SSM_GROUPS, SSM_STATE)


def kernel(x_prompt, x_sample, state_s5_re, state_s5_im, cache_kv, a_norm_pre, a_w_in, a_lam_re, a_lam_im, a_log_dt, a_b_re, a_b_im, a_c_re, a_c_im, a_d, a_w_glu, a_b_glu, a_w_out, a_norm_post, kv_norm, w_kv, b_norm_pre, b_w_in, b_w_out, b_norm_post):
    bf16 = jnp.bfloat16
    n_a = a_w_in.shape[0]
    n_b = b_w_in.shape[0]
    bsz, seq, _ = x_prompt.shape
    dbsz = x_sample.shape[0]
    n_pos = cache_kv.shape[1]

    a_w_in_b = a_w_in.astype(bf16)
    a_w_glu_b = a_w_glu.astype(bf16)
    a_w_out_b = a_w_out.astype(bf16)
    w_kv_b = w_kv.astype(bf16)
    w_kv_t = w_kv_b.T
    b_w_in_b = b_w_in.astype(bf16)
    b_w_out_b = b_w_out.astype(bf16)
    slopes = _alibi_slopes()
    ssm_ws = [_ssm_weights(a_lam_re[l], a_lam_im[l], a_log_dt[l], a_b_re[l], a_b_im[l],
                           a_c_re[l], a_c_im[l], a_d[l]) for l in range(n_a)]

    x = x_prompt
    zeros = jnp.zeros((bsz, SSM_GROUPS * SSM_STATE), jnp.float32)
    p_re, p_im = [], []
    for l in range(n_a):
        x, h_re, h_im = _ssm_layer(x, zeros, zeros, a_norm_pre[l], a_w_in_b[l], ssm_ws[l], a_w_glu_b[l],
                                   a_b_glu[l], a_w_out_b[l], a_norm_post[l], n_steps=64)
        p_re.append(_states_out(h_re, bsz))
        p_im.append(_states_out(h_im, bsz))
    kv, kv_t = kv_project(x, kv_norm, w_kv_b, w_kv_t)
    for l in range(n_b):
        proj = norm_matmul(x, b_norm_pre[l], b_w_in_b[l], time_major_out=False)
        o = prompt_attention(slopes, proj, kv)
        x = out_proj(o, b_w_out_b[l], b_norm_post[l], x, time_major_in=False)
    y_prompt = x
    prompt_kv = jnp.transpose(kv_t.reshape(bsz, 2, N_HEADS, HEAD_DIM, seq), (0, 4, 1, 2, 3))

    xs = x_sample.reshape(1, dbsz, D_MODEL)
    s_re, s_im = [], []
    for l in range(n_a):
        h0_re = state_s5_re[l].reshape(dbsz, SSM_GROUPS * SSM_STATE)
        h0_im = state_s5_im[l].reshape(dbsz, SSM_GROUPS * SSM_STATE)
        xs, h_re, h_im = _ssm_layer(xs, h0_re, h0_im, a_norm_pre[l], a_w_in_b[l], ssm_ws[l], a_w_glu_b[l],
                                    a_b_glu[l], a_w_out_b[l], a_norm_post[l], n_steps=1)
        s_re.append(_states_out(h_re, dbsz))
        s_im.append(_states_out(h_im, dbsz))
    kv_s, kv_s_t = kv_project(xs, kv_norm, w_kv_b, w_kv_t)
    cache_t = jnp.transpose(cache_kv, (0, 2, 3, 4, 1)).reshape(dbsz, 2 * ATTN_WIDTH, n_pos)
    new_cache_t = cache_shift(cache_t, kv_s_t[0])
    sample_kv = jnp.transpose(new_cache_t.reshape(dbsz, 2, N_HEADS, HEAD_DIM, n_pos), (0, 4, 1, 2, 3))

    n_hb = N_HEADS // HEADS_PER_STEP
    kv_new = kv_s[0].reshape(dbsz, 2, n_hb, HEADS_PER_STEP, HEAD_DIM)
    kv_cols = jnp.transpose(kv_new, (0, 2, 4, 1, 3)).reshape(dbsz, n_hb, HEAD_DIM, 2 * HEADS_PER_STEP)
    xs2 = xs[0]
    for l in range(n_b):
        proj = norm_matmul(xs2[None], b_norm_pre[l], b_w_in_b[l], time_major_out=False)[0]
        q = proj[:, :N_GROUPS * ATTN_WIDTH].reshape(dbsz, N_GROUPS, n_hb, HEADS_PER_STEP, HEAD_DIM)
        q_cols = jnp.transpose(q, (0, 2, 4, 1, 3)).reshape(dbsz, n_hb, HEAD_DIM, N_GROUPS * HEADS_PER_STEP)
        cols = jnp.concatenate([q_cols, kv_cols], axis=-1)
        o_t = sample_attention(slopes, cols, cache_t)
        o = jnp.transpose(o_t, (0, 1, 3, 2)).reshape(dbsz, ATTN_WIDTH)
        xs2 = gated_out_proj(o, proj, N_GROUPS, b_w_out_b[l], b_norm_post[l], xs2)
    y_sample = xs2.reshape(dbsz, 1, D_MODEL)

    return (y_prompt, y_sample, jnp.stack(p_re), jnp.stack(p_im), prompt_kv,
            jnp.stack(s_re), jnp.stack(s_im), sample_kv)
```

```python
import functools
import math

import jax
import jax.numpy as jnp
from jax import lax
from jax.experimental import pallas as pl
from jax.experimental.pallas import tpu as pltpu

D_MODEL = 1024
SSM_WIDTH = 2048
SSM_GROUP = 16
SSM_GROUPS = SSM_WIDTH // SSM_GROUP
SSM_STATE = 64
HEAD_DIM = 64
N_HEADS = 16
ATTN_WIDTH = N_HEADS * HEAD_DIM
DILATIONS = (1, 4, 16)
N_STRIDES = 128
N_GROUPS = len(DILATIONS)
ALIBI_MAX_EXP = 8.0
RMS_EPS = 1e-6
NEG_INF = -1e30

VMEM_LIMIT_BYTES_V7X = 56 * 1024 * 1024
LANES = 128

GROUPS_PER_BLOCK = 16
N_GROUP_BLOCKS = SSM_GROUPS // GROUPS_PER_BLOCK
BLOCK_CHANNELS = GROUPS_PER_BLOCK * SSM_GROUP
BLOCK_STATES = GROUPS_PER_BLOCK * SSM_STATE

ROW_TILE = 512


def _params(*semantics):
    return pltpu.CompilerParams(dimension_semantics=semantics,
                                vmem_limit_bytes=VMEM_LIMIT_BYTES_V7X)


def _rms_scale(x):
    return lax.rsqrt(jnp.mean(x * x, axis=-1, keepdims=True) + RMS_EPS)


def _norm_matmul_kernel(x_ref, g_ref, w_ref, o_ref):
    x = x_ref[...]
    xn = (x * _rms_scale(x) * g_ref[...]).astype(jnp.bfloat16)
    o_ref[...] = jnp.dot(xn, w_ref[...], preferred_element_type=jnp.float32)


def norm_matmul(x, g, w):
    rows, d = x.shape
    n = w.shape[1]
    tm = min(ROW_TILE, rows)
    return pl.pallas_call(
        _norm_matmul_kernel,
        grid=(rows // tm,),
        in_specs=[
            pl.BlockSpec((tm, d), lambda i: (i, 0)),
            pl.BlockSpec((1, d), lambda i: (0, 0)),
            pl.BlockSpec((d, n), lambda i: (0, 0)),
        ],
        out_specs=pl.BlockSpec((tm, n), lambda i: (i, 0)),
        out_shape=jax.ShapeDtypeStruct((rows, n), jnp.float32),
        compiler_params=_params("parallel"),
    )(x, g.reshape(1, d), w)


def _kv_kernel(x_ref, g_ref, w_ref, wt_ref, kv_ref, kvt_ref):
    x = x_ref[...]
    xn = (x * _rms_scale(x) * g_ref[...]).astype(jnp.bfloat16)
    kv_ref[...] = jnp.dot(xn, w_ref[...], preferred_element_type=jnp.float32)
    kvt_ref[...] = lax.dot_general(wt_ref[...], xn, (((1,), (1,)), ((), ())),
                                   preferred_element_type=jnp.float32)


def kv_project(x, g, w, wt, *, tm=256):
    bsz, seq, d = x.shape
    n = w.shape[1]
    tm = min(tm, seq)
    return pl.pallas_call(
        _kv_kernel,
        grid=(bsz, seq // tm),
        in_specs=[
            pl.BlockSpec((None, tm, d), lambda b, i: (b, i, 0)),
            pl.BlockSpec((1, d), lambda b, i: (0, 0)),
            pl.BlockSpec((d, n), lambda b, i: (0, 0)),
            pl.BlockSpec((n, d), lambda b, i: (0, 0)),
        ],
        out_specs=[
            pl.BlockSpec((None, tm, n), lambda b, i: (b, i, 0)),
            pl.BlockSpec((None, n, tm), lambda b, i: (b, 0, i)),
        ],
        out_shape=[
            jax.ShapeDtypeStruct((bsz, seq, n), jnp.float32),
            jax.ShapeDtypeStruct((bsz, n, seq), jnp.float32),
        ],
        compiler_params=_params("parallel", "parallel"),
    )(x, g.reshape(1, d), w, wt)


def _discretize_kernel(lam_re_ref, lam_im_ref, log_dt_ref, bt_re_ref, bt_im_ref,
                       ab_re_ref, ab_im_ref, bbt_re_ref, bbt_im_ref):
    lam_re = lam_re_ref[...]
    lam_im = lam_im_ref[...]
    dt = jnp.exp(log_dt_ref[...])
    mag = jnp.exp(lam_re * dt)
    ab_re = mag * jnp.cos(lam_im * dt)
    ab_im = mag * jnp.sin(lam_im * dt)
    den = lam_re * lam_re + lam_im * lam_im
    num_re = ab_re - 1.0
    coef_re = (num_re * lam_re + ab_im * lam_im) / den
    coef_im = (ab_im * lam_re - num_re * lam_im) / den
    ab_re_ref[...] = ab_re
    ab_im_ref[...] = ab_im
    b_re = bt_re_ref[...]
    b_im = bt_im_ref[...]
    cr = coef_re[:, None, :]
    ci = coef_im[:, None, :]
    bbt_re_ref[...] = cr * b_re - ci * b_im
    bbt_im_ref[...] = cr * b_im + ci * b_re


def discretize(lam_re, lam_im, log_dt, b_re, b_im):
    g, p = lam_re.shape
    gc = b_re.shape[-1]
    bt_re = jnp.swapaxes(b_re, 1, 2)
    bt_im = jnp.swapaxes(b_im, 1, 2)
    f32 = jnp.float32
    return pl.pallas_call(
        _discretize_kernel,
        out_shape=[jax.ShapeDtypeStruct((g, p), f32), jax.ShapeDtypeStruct((g, p), f32),
                   jax.ShapeDtypeStruct((g, gc, p), f32), jax.ShapeDtypeStruct((g, gc, p), f32)],
    )(lam_re, lam_im, log_dt.reshape(g, 1), bt_re, bt_im)


def _block_diag(w, rows_first):
    gc, p = w.shape[1], w.shape[2]
    w4 = w.reshape(N_GROUP_BLOCKS, GROUPS_PER_BLOCK, gc, p)
    eye = jnp.eye(GROUPS_PER_BLOCK, dtype=bool)
    if rows_first:
        bd = jnp.where(eye[None, :, None, :, None], w4[:, :, :, None, :], 0.0)
        return bd.reshape(N_GROUP_BLOCKS, GROUPS_PER_BLOCK * gc, GROUPS_PER_BLOCK * p)
    wt = jnp.swapaxes(w4, 2, 3)
    bd = jnp.where(eye[None, :, None, :, None], wt[:, :, :, None, :], 0.0)
    return bd.reshape(N_GROUP_BLOCKS, GROUPS_PER_BLOCK * p, GROUPS_PER_BLOCK * gc)


def _ssm_kernel(u_ref, bre_ref, bim_ref, cre_ref, cim_ref, ar_ref, ai_ref, d_ref,
                h0re_ref, h0im_ref, g_ref, hre_out_ref, him_out_ref,
                sre_ref, sim_ref, hre_ref, him_ref, *, n_steps, bsz):
    tc = pl.program_id(1)

    @pl.when(tc == 0)
    def _():
        hre_ref[...] = h0re_ref[...]
        him_ref[...] = h0im_ref[...]

    u = u_ref[...]
    ub = u.astype(jnp.bfloat16)
    sre_ref[...] = jnp.dot(ub, bre_ref[...], preferred_element_type=jnp.float32)
    sim_ref[...] = jnp.dot(ub, bim_ref[...], preferred_element_type=jnp.float32)

    ar = jnp.broadcast_to(ar_ref[...], (bsz, BLOCK_STATES))
    ai = jnp.broadcast_to(ai_ref[...], (bsz, BLOCK_STATES))

    def step(t, carry):
        hr, hi = carry
        rows = pl.ds(pl.multiple_of(t * bsz, bsz), bsz)
        nhr = ar * hr - ai * hi + sre_ref[rows, :]
        nhi = ar * hi + ai * hr + sim_ref[rows, :]
        sre_ref[rows, :] = nhr
        sim_ref[rows, :] = nhi
        return nhr, nhi

    carry = (hre_ref[...], him_ref[...])
    if n_steps == 1:
        hr, hi = step(0, carry)
    else:
        hr, hi = lax.fori_loop(0, n_steps, step, carry, unroll=2)
    hre_ref[...] = hr
    him_ref[...] = hi

    y = jnp.dot(sre_ref[...].astype(jnp.bfloat16), cre_ref[...], preferred_element_type=jnp.float32)
    y = y - jnp.dot(sim_ref[...].astype(jnp.bfloat16), cim_ref[...], preferred_element_type=jnp.float32)
    y = y + d_ref[...] * u
    g_ref[...] = jax.nn.gelu(y).astype(jnp.bfloat16)

    @pl.when(tc == pl.num_programs(1) - 1)
    def _():
        hre_out_ref[...] = hr
        him_out_ref[...] = hi


def ssm_core(uz, bsz, bre, bim, cre, cim, ar, ai, d_skip, h0_re, h0_im, *, n_steps):
    rows = uz.shape[0]
    seq = rows // bsz
    n_steps = min(n_steps, seq)
    tr = n_steps * bsz
    kern = functools.partial(_ssm_kernel, n_steps=n_steps, bsz=bsz)
    f32 = jnp.float32
    wspec = lambda shape: pl.BlockSpec((None,) + shape, lambda gb, tc: (gb, 0, 0))
    return pl.pallas_call(
        kern,
        grid=(N_GROUP_BLOCKS, seq // n_steps),
        in_specs=[
            pl.BlockSpec((tr, BLOCK_CHANNELS), lambda gb, tc: (tc, gb)),
            wspec((BLOCK_CHANNELS, BLOCK_STATES)), wspec((BLOCK_CHANNELS, BLOCK_STATES)),
            wspec((BLOCK_STATES, BLOCK_CHANNELS)), wspec((BLOCK_STATES, BLOCK_CHANNELS)),
            wspec((1, BLOCK_STATES)), wspec((1, BLOCK_STATES)), wspec((1, BLOCK_CHANNELS)),
            pl.BlockSpec((bsz, BLOCK_STATES), lambda gb, tc: (0, gb)),
            pl.BlockSpec((bsz, BLOCK_STATES), lambda gb, tc: (0, gb)),
        ],
        out_specs=[
            pl.BlockSpec((tr, BLOCK_CHANNELS), lambda gb, tc: (tc, gb)),
            pl.BlockSpec((bsz, BLOCK_STATES), lambda gb, tc: (0, gb)),
            pl.BlockSpec((bsz, BLOCK_STATES), lambda gb, tc: (0, gb)),
        ],
        out_shape=[
            jax.ShapeDtypeStruct((rows, SSM_WIDTH), jnp.bfloat16),
            jax.ShapeDtypeStruct((bsz, SSM_GROUPS * SSM_STATE), f32),
            jax.ShapeDtypeStruct((bsz, SSM_GROUPS * SSM_STATE), f32),
        ],
        scratch_shapes=[pltpu.VMEM((tr, BLOCK_STATES), f32), pltpu.VMEM((tr, BLOCK_STATES), f32),
                        pltpu.VMEM((bsz, BLOCK_STATES), f32), pltpu.VMEM((bsz, BLOCK_STATES), f32)],
        compiler_params=_params("parallel", "arbitrary"),
    )(uz, bre, bim, cre, cim, ar, ai, d_skip, h0_re, h0_im)


def _glu_kernel(g_ref, wa_ref, wb_ref, ba_ref, bb_ref, z_ref, o_ref):
    g = g_ref[...]
    a = jnp.dot(g, wa_ref[...], preferred_element_type=jnp.float32) + ba_ref[...]
    b = jnp.dot(g, wb_ref[...], preferred_element_type=jnp.float32) + bb_ref[...]
    o_ref[...] = (a * jax.nn.sigmoid(b) * jax.nn.silu(z_ref[...])).astype(jnp.bfloat16)


def glu(g, w_glu, b_glu, uz, *, tn=512):
    rows, e = g.shape
    tm = min(ROW_TILE, rows)
    nj = e // tn
    return pl.pallas_call(
        _glu_kernel,
        grid=(nj, rows // tm),
        in_specs=[
            pl.BlockSpec((tm, e), lambda j, i: (i, 0)),
            pl.BlockSpec((e, tn), lambda j, i: (0, j)),
            pl.BlockSpec((e, tn), lambda j, i: (0, nj + j)),
            pl.BlockSpec((1, tn), lambda j, i: (0, j)),
            pl.BlockSpec((1, tn), lambda j, i: (0, nj + j)),
            pl.BlockSpec((tm, tn), lambda j, i: (i, nj + j)),
        ],
        out_specs=pl.BlockSpec((tm, tn), lambda j, i: (i, j)),
        out_shape=jax.ShapeDtypeStruct((rows, e), jnp.bfloat16),
        compiler_params=_params("parallel", "parallel"),
    )(g, w_glu, w_glu, b_glu, b_glu, uz)


def _out_proj_kernel(h_ref, w_ref, g_ref, x_ref, o_ref):
    y = jnp.dot(h_ref[...], w_ref[...], preferred_element_type=jnp.float32)
    o_ref[...] = x_ref[...] + y * _rms_scale(y) * g_ref[...]


def _gated_out_proj_kernel(a_ref, z_ref, w_ref, g_ref, x_ref, o_ref):
    h = (a_ref[...] * jax.nn.silu(z_ref[...])).astype(jnp.bfloat16)
    y = jnp.dot(h, w_ref[...], preferred_element_type=jnp.float32)
    o_ref[...] = x_ref[...] + y * _rms_scale(y) * g_ref[...]


def out_proj(h, w, g, x):
    rows, d = x.shape
    k = w.shape[0]
    tm = min(ROW_TILE, rows)
    return pl.pallas_call(
        _out_proj_kernel,
        grid=(rows // tm,),
        in_specs=[
            pl.BlockSpec((tm, k), lambda i: (i, 0)),
            pl.BlockSpec((k, d), lambda i: (0, 0)),
            pl.BlockSpec((1, d), lambda i: (0, 0)),
            pl.BlockSpec((tm, d), lambda i: (i, 0)),
        ],
        out_specs=pl.BlockSpec((tm, d), lambda i: (i, 0)),
        out_shape=jax.ShapeDtypeStruct((rows, d), jnp.float32),
        compiler_params=_params("parallel"),
    )(h, w, g.reshape(1, d), x)


def gated_out_proj(a, proj, z_block, w, g, x):
    rows, d = x.shape
    k = w.shape[0]
    return pl.pallas_call(
        _gated_out_proj_kernel,
        grid=(1,),
        in_specs=[
            pl.BlockSpec((rows, k), lambda i: (0, 0)),
            pl.BlockSpec((rows, k), lambda i: (0, z_block)),
            pl.BlockSpec((k, d), lambda i: (0, 0)),
            pl.BlockSpec((1, d), lambda i: (0, 0)),
            pl.BlockSpec((rows, d), lambda i: (0, 0)),
        ],
        out_specs=pl.BlockSpec((rows, d), lambda i: (0, 0)),
        out_shape=jax.ShapeDtypeStruct((rows, d), jnp.float32),
        compiler_params=_params("arbitrary"),
    )(a, proj, w, g.reshape(1, d), x)


QB = N_STRIDES
HEAD_PAIR = 2 * HEAD_DIM
LOG2_E = math.log2(math.e)
ATTN_UNROLL = 4


def _prompt_attn_kernel(slopes_ref, q0_ref, q1_ref, q2_ref, z_ref, k_ref, v_ref, o_ref,
                        acc0, acc1, acc2, m0, m1, m2, l0, l1, l2, va0_ref, va1_ref, bias_ref, *, seq):
    hp = pl.program_id(1)
    q_refs = (q0_ref, q1_ref, q2_ref)
    accs, ms, ls = (acc0, acc1, acc2), (m0, m1, m2), (l0, l1, l2)
    first_head = lax.broadcasted_iota(jnp.int32, (QB, HEAD_PAIR), 1) < HEAD_DIM
    nt = (((1,), (1,)), ((), ()))
    blocks_per_group = seq // QB
    bf16 = jnp.bfloat16
    q_scale = LOG2_E / math.sqrt(HEAD_DIM)

    def fill_values(i, _):
        rows = pl.ds(pl.multiple_of(i * QB, QB), QB)
        v = v_ref[rows, :]
        va0_ref[rows, :] = jnp.where(first_head, v, 1.0)
        va1_ref[rows, :] = jnp.where(first_head, 1.0, v)
        return 0

    lax.fori_loop(0, seq // QB, fill_values, 0, unroll=2)

    for g, dil in enumerate(DILATIONS):
        blocks_per_res = seq // dil // QB
        has_prev_block = blocks_per_res > 1
        n_keys = 2 * QB if has_prev_block else QB

        qi = lax.broadcasted_iota(jnp.int32, (QB, n_keys), 0)
        kj = lax.broadcasted_iota(jnp.int32, (QB, n_keys), 1)
        dist = qi + (n_keys - QB) - kj
        valid = jnp.logical_and(dist >= 0, dist <= N_STRIDES)
        distf = dist.astype(jnp.float32)
        for hh in range(2):
            slope = slopes_ref[g * N_HEADS + 2 * hp + hh] * (float(dil) * LOG2_E)
            table = jnp.where(valid, -slope * distf, NEG_INF)
            bias_ref[1, hh * QB:(hh + 1) * QB, 0:n_keys] = table
            if has_prev_block:
                bias_ref[0, hh * QB:(hh + 1) * QB, 0:n_keys] = jnp.where(kj >= QB, table, NEG_INF)

        def body(j, _, g=g, dil=dil, has_prev_block=has_prev_block, n_keys=n_keys):
            if dil == 1:
                start = pl.multiple_of(j * QB, QB)
                prev_start = jnp.maximum(start - QB, 0)
                rows = lambda s: pl.ds(s, QB)
                has_prev = j > 0
            else:
                blk = j // dil
                res = j - blk * dil
                start = res + blk * (QB * dil)
                prev_start = jnp.maximum(start - QB * dil, 0)
                rows = lambda s: pl.ds(s, QB, stride=dil)
                has_prev = blk > 0
            q2 = q_refs[g][rows(start), :] * q_scale
            qs = jnp.concatenate([jnp.where(first_head, q2, 0.0), jnp.where(first_head, 0.0, q2)], axis=0)
            if has_prev_block:
                kk = jnp.concatenate([k_ref[rows(prev_start), :], k_ref[rows(start), :]], axis=0)
                v_aug = jnp.concatenate(
                    [jnp.concatenate([va0_ref[rows(prev_start), :], va0_ref[rows(start), :]], axis=0),
                     jnp.concatenate([va1_ref[rows(prev_start), :], va1_ref[rows(start), :]], axis=0)], axis=1)
                bias = bias_ref[has_prev.astype(jnp.int32)]
            else:
                kk = k_ref[rows(start), :]
                v_aug = jnp.concatenate([va0_ref[rows(start), :], va1_ref[rows(start), :]], axis=1)
                bias = bias_ref[1, :, 0:n_keys]
            s = lax.dot_general(qs.astype(bf16), kk.astype(bf16), nt, preferred_element_type=jnp.float32)
            s = s + bias
            mx = jnp.max(s, axis=-1, keepdims=True)
            p = jnp.exp2(s - mx).astype(bf16)
            r = jnp.dot(p, v_aug.astype(bf16), preferred_element_type=jnp.float32)
            r0 = r[:QB, :HEAD_PAIR]
            r1 = r[QB:, HEAD_PAIR:]
            accs[g][rows(start), :] = jnp.where(first_head, r0, r1)
            ls[g][rows(start), :] = pltpu.roll(jnp.where(first_head, r1, r0), HEAD_DIM, 1)
            ms[g][rows(start), :] = jnp.where(first_head, jnp.broadcast_to(mx[:QB], (QB, HEAD_PAIR)),
                                              jnp.broadcast_to(mx[QB:], (QB, HEAD_PAIR)))
            return 0

        lax.fori_loop(0, blocks_per_group, body, 0, unroll=ATTN_UNROLL)

    def merge(i, _):
        rows = pl.ds(pl.multiple_of(i * QB, QB), QB)
        mm = jnp.maximum(jnp.maximum(m0[rows, :], m1[rows, :]), m2[rows, :])
        num = jnp.zeros((QB, HEAD_PAIR), jnp.float32)
        den = jnp.zeros((QB, HEAD_PAIR), jnp.float32)
        for g in range(N_GROUPS):
            e = jnp.exp2(ms[g][rows, :] - mm)
            num = num + e * accs[g][rows, :]
            den = den + e * ls[g][rows, :]
        o_ref[rows, :] = ((num / den) * jax.nn.silu(z_ref[rows, :])).astype(jnp.bfloat16)
        return 0

    lax.fori_loop(0, seq // QB, merge, 0, unroll=2)


def prompt_attention(slopes, proj, kv):
    bsz, seq, _ = proj.shape
    n_pairs = ATTN_WIDTH // HEAD_PAIR
    col = lambda off: pl.BlockSpec((None, seq, HEAD_PAIR), lambda b, hp, off=off: (b, 0, off + hp))
    f32 = jnp.float32
    scratch = [pltpu.VMEM((seq, HEAD_PAIR), f32) for _ in range(11)]
    scratch.append(pltpu.VMEM((2, 2 * QB, 2 * QB), f32))
    return pl.pallas_call(
        functools.partial(_prompt_attn_kernel, seq=seq),
        grid=(bsz, n_pairs),
        in_specs=[pl.BlockSpec(memory_space=pltpu.SMEM),
                  col(0), col(n_pairs), col(2 * n_pairs), col(3 * n_pairs), col(0), col(n_pairs)],
        out_specs=pl.BlockSpec((None, seq, HEAD_PAIR), lambda b, hp: (b, 0, hp)),
        out_shape=jax.ShapeDtypeStruct((bsz, seq, ATTN_WIDTH), jnp.bfloat16),
        scratch_shapes=scratch,
        compiler_params=_params("parallel", "parallel"),
    )(slopes, proj, proj, proj, proj, kv, kv)


def _select_positions(x, lo, dil):
    width = x.shape[1] - lo
    xs = x[:, lo:]
    hi = xs.astype(jnp.bfloat16)
    rem = (xs - hi.astype(jnp.float32)).astype(jnp.bfloat16)
    pos = lax.broadcasted_iota(jnp.int32, (width, LANES), 0)
    col = lax.broadcasted_iota(jnp.int32, (width, LANES), 1)
    sel = jnp.where(pos == col * dil, 1.0, 0.0).astype(jnp.bfloat16)
    return (jnp.dot(hi, sel, preferred_element_type=jnp.float32)
            + jnp.dot(rem, sel, preferred_element_type=jnp.float32))


def _cache_shift_kernel(c_ref, new_ref, o_ref, *sel_refs):
    b = pl.program_id(0)
    rows, n_pos = c_ref.shape
    x = c_ref[...]
    o_ref[...] = pltpu.roll(x, n_pos - 1, 1)
    new = new_ref[...]
    lane = lax.broadcasted_iota(jnp.int32, new.shape, 1)
    new_col = jnp.sum(jnp.where(lane == b, new, 0.0), axis=1, keepdims=True)
    last = o_ref[:, n_pos - LANES:]
    last_lane = lax.broadcasted_iota(jnp.int32, last.shape, 1) == LANES - 1
    o_ref[:, n_pos - LANES:] = jnp.where(last_lane, new_col, last)
    for ref, dil in zip(sel_refs, DILATIONS[1:]):
        ref[...] = _select_positions(x, n_pos - N_STRIDES * dil, dil)


def cache_shift(cache_t, new_t, *, rows=512):
    bsz, chans, n_pos = cache_t.shape
    n_sel = len(DILATIONS) - 1
    sel_spec = pl.BlockSpec((None, rows, LANES), lambda b, i: (b, i, 0))
    sel_shape = jax.ShapeDtypeStruct((bsz, chans, LANES), cache_t.dtype)
    return pl.pallas_call(
        _cache_shift_kernel,
        grid=(bsz, chans // rows),
        in_specs=[pl.BlockSpec((None, rows, n_pos), lambda b, i: (b, i, 0)),
                  pl.BlockSpec((rows, bsz), lambda b, i: (i, 0))],
        out_specs=[pl.BlockSpec((None, rows, n_pos), lambda b, i: (b, i, 0))] + [sel_spec] * n_sel,
        out_shape=[jax.ShapeDtypeStruct(cache_t.shape, cache_t.dtype)] + [sel_shape] * n_sel,
        compiler_params=_params("parallel", "parallel"),
    )(cache_t, new_t)


HEADS_PER_BLOCK = 4


def _sample_attn_kernel(slopes_ref, cols_ref, c1_ref, c4_ref, c16_ref, o_ref):
    packed = (c1_ref, c4_ref, c16_ref)
    lane = lax.broadcasted_iota(jnp.int32, (1, LANES), 1)
    steps = (N_STRIDES - lane).astype(jnp.float32)
    n_q = N_GROUPS * HEADS_PER_BLOCK

    def head_block(hb, _):
        cols = cols_ref[hb]
        for hh in range(HEADS_PER_BLOCK):
            k_rows = pl.ds(pl.multiple_of((hb * HEADS_PER_BLOCK + hh) * HEAD_DIM, HEAD_DIM), HEAD_DIM)
            v_rows = pl.ds(pl.multiple_of(ATTN_WIDTH + (hb * HEADS_PER_BLOCK + hh) * HEAD_DIM, HEAD_DIM),
                           HEAD_DIM)
            k_new = cols[:, n_q + hh:n_q + hh + 1]
            v_new = cols[:, n_q + HEADS_PER_BLOCK + hh:n_q + HEADS_PER_BLOCK + hh + 1]
            scores, new_scores = [], []
            for g, dil in enumerate(DILATIONS):
                q = cols[:, g * HEADS_PER_BLOCK + hh:g * HEADS_PER_BLOCK + hh + 1] * (1.0 / math.sqrt(HEAD_DIM))
                slope = slopes_ref[g * N_HEADS + hb * HEADS_PER_BLOCK + hh] * float(dil)
                s = jnp.sum(packed[g][k_rows, :] * q, axis=0, keepdims=True) - slope * steps
                scores.append(s)
                new_scores.append(jnp.sum(k_new * q, axis=0, keepdims=True))
            mx = jnp.maximum(jnp.maximum(new_scores[0], new_scores[1]), new_scores[2])
            for g in range(N_GROUPS):
                mx = jnp.maximum(mx, jnp.max(scores[g], axis=1, keepdims=True))
            p_new = sum(jnp.exp(s - mx) for s in new_scores)
            den = p_new
            o = p_new * v_new
            for g in range(N_GROUPS):
                p = jnp.exp(scores[g] - mx)
                den = den + jnp.sum(p, axis=1, keepdims=True)
                o = o + jnp.sum(packed[g][v_rows, :] * p, axis=1, keepdims=True)
            o_ref[hb, :, hh:hh + 1] = o / den
        return 0

    lax.fori_loop(0, N_HEADS // HEADS_PER_BLOCK, head_block, 0)


def sample_attention(slopes, cols, cache_t, c4, c16):
    bsz, chans, n_pos = cache_t.shape
    n_hb = N_HEADS // HEADS_PER_BLOCK
    ncol = cols.shape[-1]
    return pl.pallas_call(
        _sample_attn_kernel,
        grid=(bsz,),
        in_specs=[pl.BlockSpec(memory_space=pltpu.SMEM),
                  pl.BlockSpec((None, n_hb, HEAD_DIM, ncol), lambda b: (b, 0, 0, 0)),
                  pl.BlockSpec((None, chans, LANES), lambda b: (b, 0, n_pos // LANES - 1)),
                  pl.BlockSpec((None, chans, LANES), lambda b: (b, 0, 0)),
                  pl.BlockSpec((None, chans, LANES), lambda b: (b, 0, 0))],
        out_specs=pl.BlockSpec((None, n_hb, HEAD_DIM, HEADS_PER_BLOCK), lambda b: (b, 0, 0, 0)),
        out_shape=jax.ShapeDtypeStruct((bsz, n_hb, HEAD_DIM, HEADS_PER_BLOCK), jnp.float32),
        compiler_params=_params("parallel"),
    )(slopes, cols, cache_t, c4, c16)


def _alibi_slopes():
    n = N_GROUPS * N_HEADS
    e = jnp.arange(1, n + 1, dtype=jnp.float32)
    return jnp.exp2(-ALIBI_MAX_EXP * e / n)


def _ssm_weights(lam_re, lam_im, log_dt, b_re, b_im, c_re, c_im, d_skip):
    ab_re, ab_im, bbt_re, bbt_im = discretize(lam_re, lam_im, log_dt, b_re, b_im)
    bf16 = jnp.bfloat16
    return dict(
        bre=_block_diag(bbt_re, True).astype(bf16), bim=_block_diag(bbt_im, True).astype(bf16),
        cre=_block_diag(c_re, False).astype(bf16), cim=_block_diag(c_im, False).astype(bf16),
        ar=ab_re.reshape(N_GROUP_BLOCKS, 1, BLOCK_STATES), ai=ab_im.reshape(N_GROUP_BLOCKS, 1, BLOCK_STATES),
        d_skip=d_skip.reshape(N_GROUP_BLOCKS, 1, BLOCK_CHANNELS))


def _ssm_layer(x, h0_re, h0_im, norm_pre, w_in, ssm_w, w_glu, b_glu, w_out, norm_post, *, n_steps):
    uz = norm_matmul(x, norm_pre, w_in)
    g, h_re, h_im = ssm_core(uz, h0_re.shape[0], h0_re=h0_re, h0_im=h0_im, n_steps=n_steps, **ssm_w)
    h2 = glu(g, w_glu, b_glu.reshape(1, -1), uz)
    return out_proj(h2, w_out, norm_post, x), h_re, h_im


def _states_out(h, bsz):
    return h.reshape(bsz, SSM_GROUPS, SSM_STATE)


def kernel(x_prompt, x_sample, state_s5_re, state_s5_im, cache_kv, a_norm_pre, a_w_in, a_lam_re, a_lam_im, a_log_dt, a_b_re, a_b_im, a_c_re, a_c_im, a_d, a_w_glu, a_b_glu, a_w_out, a_norm_post, kv_norm, w_kv, b_norm_pre, b_w_in, b_w_out, b_norm_post):
    bf16 = jnp.bfloat16
    n_a = a_w_in.shape[0]
    n_b = b_w_in.shape[0]
    bsz, seq, _ = x_prompt.shape
    dbsz = x_sample.shape[0]
    n_pos = cache_kv.shape[1]

    a_w_in_b = a_w_in.astype(bf16)
    a_w_glu_b = a_w_glu.astype(bf16)
    a_w_out_b = a_w_out.astype(bf16)
    w_kv_b = w_kv.astype(bf16)
    w_kv_t = w_kv_b.T
    b_w_in_b = b_w_in.astype(bf16)
    b_w_out_b = b_w_out.astype(bf16)
    slopes = _alibi_slopes()
    ssm_ws = [_ssm_weights(a_lam_re[l], a_lam_im[l], a_log_dt[l], a_b_re[l], a_b_im[l],
                           a_c_re[l], a_c_im[l], a_d[l]) for l in range(n_a)]

    x = jnp.swapaxes(x_prompt, 0, 1).reshape(seq * bsz, D_MODEL)
    zeros = jnp.zeros((bsz, SSM_GROUPS * SSM_STATE), jnp.float32)
    p_re, p_im = [], []
    for l in range(n_a):
        x, h_re, h_im = _ssm_layer(x, zeros, zeros, a_norm_pre[l], a_w_in_b[l], ssm_ws[l], a_w_glu_b[l],
                                   a_b_glu[l], a_w_out_b[l], a_norm_post[l], n_steps=64)
        p_re.append(_states_out(h_re, bsz))
        p_im.append(_states_out(h_im, bsz))
    x = jnp.swapaxes(x.reshape(seq, bsz, D_MODEL), 0, 1)
    kv, kv_t = kv_project(x, kv_norm, w_kv_b, w_kv_t)
    x = x.reshape(bsz * seq, D_MODEL)
    for l in range(n_b):
        proj = norm_matmul(x, b_norm_pre[l], b_w_in_b[l])
        o = prompt_attention(slopes, proj.reshape(bsz, seq, -1), kv)
        x = out_proj(o.reshape(bsz * seq, ATTN_WIDTH), b_w_out_b[l], b_norm_post[l], x)
    y_prompt = x.reshape(bsz, seq, D_MODEL)
    prompt_kv = jnp.transpose(kv_t.reshape(bsz, 2, N_HEADS, HEAD_DIM, seq), (0, 4, 1, 2, 3))

    xs = x_sample.reshape(dbsz, D_MODEL)
    s_re, s_im = [], []
    for l in range(n_a):
        h0_re = state_s5_re[l].reshape(dbsz, SSM_GROUPS * SSM_STATE)
        h0_im = state_s5_im[l].reshape(dbsz, SSM_GROUPS * SSM_STATE)
        xs, h_re, h_im = _ssm_layer(xs, h0_re, h0_im, a_norm_pre[l], a_w_in_b[l], ssm_ws[l], a_w_glu_b[l],
                                    a_b_glu[l], a_w_out_b[l], a_norm_post[l], n_steps=1)
        s_re.append(_states_out(h_re, dbsz))
        s_im.append(_states_out(h_im, dbsz))
    kv_s, kv_s_t = kv_project(xs[None], kv_norm, w_kv_b, w_kv_t)
    cache_t = jnp.transpose(cache_kv, (0, 2, 3, 4, 1)).reshape(dbsz, 2 * ATTN_WIDTH, n_pos)
    new_cache_t, c4, c16 = cache_shift(cache_t, kv_s_t[0])
    sample_kv = jnp.transpose(new_cache_t.reshape(dbsz, 2, N_HEADS, HEAD_DIM, n_pos), (0, 4, 1, 2, 3))

    n_hb = N_HEADS // HEADS_PER_BLOCK
    kv_new = kv_s[0].reshape(dbsz, 2, n_hb, HEADS_PER_BLOCK, HEAD_DIM)
    kv_cols = jnp.transpose(kv_new, (0, 2, 4, 1, 3)).reshape(dbsz, n_hb, HEAD_DIM, 2 * HEADS_PER_BLOCK)
    for l in range(n_b):
        proj = norm_matmul(xs, b_norm_pre[l], b_w_in_b[l])
        q = proj[:, :N_GROUPS * ATTN_WIDTH].reshape(dbsz, N_GROUPS, n_hb, HEADS_PER_BLOCK, HEAD_DIM)
        q_cols = jnp.transpose(q, (0, 2, 4, 1, 3)).reshape(dbsz, n_hb, HEAD_DIM, N_GROUPS * HEADS_PER_BLOCK)
        cols = jnp.concatenate([q_cols, kv_cols], axis=-1)
        o_t = sample_attention(slopes, cols, cache_t, c4, c16)
        o = jnp.transpose(o_t, (0, 1, 3, 2)).reshape(dbsz, ATTN_WIDTH)
        xs = gated_out_proj(o, proj, N_GROUPS, b_w_out_b[l], b_norm_post[l], xs)
    y_sample = xs.reshape(dbsz, 1, D_MODEL)

    return (y_prompt, y_sample, jnp.stack(p_re), jnp.stack(p_im), prompt_kv,
            jnp.stack(s_re), jnp.stack(s_im), sample_kv)
```

```python
import functools
import math

import jax
import jax.numpy as jnp
from jax import lax
from jax.experimental import pallas as pl
from jax.experimental.pallas import tpu as pltpu

D_MODEL = 1024
SSM_WIDTH = 2048
SSM_GROUP = 16
SSM_GROUPS = SSM_WIDTH // SSM_GROUP
SSM_STATE = 64
HEAD_DIM = 64
N_HEADS = 16
ATTN_WIDTH = N_HEADS * HEAD_DIM
DILATIONS = (1, 4, 16)
N_STRIDES = 128
N_GROUPS = len(DILATIONS)
ALIBI_MAX_EXP = 8.0
RMS_EPS = 1e-6
NEG_INF = -1e30

VMEM_LIMIT_BYTES_V7X = 56 * 1024 * 1024
LANES = 128

GROUPS_PER_BLOCK = 16
N_GROUP_BLOCKS = SSM_GROUPS // GROUPS_PER_BLOCK
BLOCK_CHANNELS = GROUPS_PER_BLOCK * SSM_GROUP
BLOCK_STATES = GROUPS_PER_BLOCK * SSM_STATE
SSM_STAGES = 4

ROW_TILE = 512


def _params(*semantics):
    return pltpu.CompilerParams(dimension_semantics=semantics,
                                vmem_limit_bytes=VMEM_LIMIT_BYTES_V7X)


def _rms_scale(x):
    return lax.rsqrt(jnp.mean(x * x, axis=-1, keepdims=True) + RMS_EPS)


def _norm_matmul_kernel(x_ref, g_ref, w_ref, o_ref):
    x = x_ref[...]
    xn = (x * _rms_scale(x) * g_ref[...]).astype(jnp.bfloat16)
    o_ref[...] = jnp.dot(xn, w_ref[...], preferred_element_type=jnp.float32)


def norm_matmul(x, g, w):
    rows, d = x.shape
    n = w.shape[1]
    tm = min(ROW_TILE, rows)
    return pl.pallas_call(
        _norm_matmul_kernel,
        grid=(rows // tm,),
        in_specs=[
            pl.BlockSpec((tm, d), lambda i: (i, 0)),
            pl.BlockSpec((1, d), lambda i: (0, 0)),
            pl.BlockSpec((d, n), lambda i: (0, 0)),
        ],
        out_specs=pl.BlockSpec((tm, n), lambda i: (i, 0)),
        out_shape=jax.ShapeDtypeStruct((rows, n), jnp.float32),
        compiler_params=_params("parallel"),
    )(x, g.reshape(1, d), w)


def _kv_kernel(x_ref, g_ref, w_ref, wt_ref, kv_ref, kvt_ref):
    x = x_ref[...]
    xn = (x * _rms_scale(x) * g_ref[...]).astype(jnp.bfloat16)
    kv_ref[...] = jnp.dot(xn, w_ref[...], preferred_element_type=jnp.float32)
    kvt_ref[...] = lax.dot_general(wt_ref[...], xn, (((1,), (1,)), ((), ())),
                                   preferred_element_type=jnp.float32)


def kv_project(x, g, w, wt, *, tm=256):
    bsz, seq, d = x.shape
    n = w.shape[1]
    tm = min(tm, seq)
    return pl.pallas_call(
        _kv_kernel,
        grid=(bsz, seq // tm),
        in_specs=[
            pl.BlockSpec((None, tm, d), lambda b, i: (b, i, 0)),
            pl.BlockSpec((1, d), lambda b, i: (0, 0)),
            pl.BlockSpec((d, n), lambda b, i: (0, 0)),
            pl.BlockSpec((n, d), lambda b, i: (0, 0)),
        ],
        out_specs=[
            pl.BlockSpec((None, tm, n), lambda b, i: (b, i, 0)),
            pl.BlockSpec((None, n, tm), lambda b, i: (b, 0, i)),
        ],
        out_shape=[
            jax.ShapeDtypeStruct((bsz, seq, n), jnp.float32),
            jax.ShapeDtypeStruct((bsz, n, seq), jnp.float32),
        ],
        compiler_params=_params("parallel", "parallel"),
    )(x, g.reshape(1, d), w, wt)


def _discretize_kernel(lam_re_ref, lam_im_ref, log_dt_ref, bt_re_ref, bt_im_ref,
                       ab_re_ref, ab_im_ref, bbt_re_ref, bbt_im_ref):
    lam_re = lam_re_ref[...]
    lam_im = lam_im_ref[...]
    dt = jnp.exp(log_dt_ref[...])
    mag = jnp.exp(lam_re * dt)
    ab_re = mag * jnp.cos(lam_im * dt)
    ab_im = mag * jnp.sin(lam_im * dt)
    den = lam_re * lam_re + lam_im * lam_im
    num_re = ab_re - 1.0
    coef_re = (num_re * lam_re + ab_im * lam_im) / den
    coef_im = (ab_im * lam_re - num_re * lam_im) / den
    ab_re_ref[...] = ab_re
    ab_im_ref[...] = ab_im
    b_re = bt_re_ref[...]
    b_im = bt_im_ref[...]
    cr = coef_re[:, None, :]
    ci = coef_im[:, None, :]
    bbt_re_ref[...] = cr * b_re - ci * b_im
    bbt_im_ref[...] = cr * b_im + ci * b_re


def discretize(lam_re, lam_im, log_dt, b_re, b_im):
    g, p = lam_re.shape
    gc = b_re.shape[-1]
    bt_re = jnp.swapaxes(b_re, 1, 2)
    bt_im = jnp.swapaxes(b_im, 1, 2)
    f32 = jnp.float32
    return pl.pallas_call(
        _discretize_kernel,
        out_shape=[jax.ShapeDtypeStruct((g, p), f32), jax.ShapeDtypeStruct((g, p), f32),
                   jax.ShapeDtypeStruct((g, gc, p), f32), jax.ShapeDtypeStruct((g, gc, p), f32)],
    )(lam_re, lam_im, log_dt.reshape(g, 1), bt_re, bt_im)


def _block_diag(w, rows_first):
    gc, p = w.shape[1], w.shape[2]
    w4 = w.reshape(N_GROUP_BLOCKS, GROUPS_PER_BLOCK, gc, p)
    eye = jnp.eye(GROUPS_PER_BLOCK, dtype=bool)
    if rows_first:
        bd = jnp.where(eye[None, :, None, :, None], w4[:, :, :, None, :], 0.0)
        return bd.reshape(N_GROUP_BLOCKS, GROUPS_PER_BLOCK * gc, GROUPS_PER_BLOCK * p)
    wt = jnp.swapaxes(w4, 2, 3)
    bd = jnp.where(eye[None, :, None, :, None], wt[:, :, :, None, :], 0.0)
    return bd.reshape(N_GROUP_BLOCKS, GROUPS_PER_BLOCK * p, GROUPS_PER_BLOCK * gc)


def _ssm_kernel(u_ref, bre_ref, bim_ref, cre_ref, cim_ref, ar_ref, ai_ref, d_ref,
                h0re_ref, h0im_ref, g_ref, hre_out_ref, him_out_ref,
                sre_ref, sim_ref, hre_ref, him_ref, *, n_steps, bsz):
    tc = pl.program_id(1)

    @pl.when(tc == 0)
    def _():
        hre_ref[...] = h0re_ref[...]
        him_ref[...] = h0im_ref[...]

    ar = jnp.broadcast_to(ar_ref[...], (bsz, BLOCK_STATES))
    ai = jnp.broadcast_to(ai_ref[...], (bsz, BLOCK_STATES))

    n_stages = min(SSM_STAGES, n_steps)
    steps_per_stage = n_steps // n_stages
    stage_rows = lambda s: slice(s * steps_per_stage * bsz, (s + 1) * steps_per_stage * bsz)

    def project_in(s):
        ub = u_ref[stage_rows(s), :].astype(jnp.bfloat16)
        sre_ref[stage_rows(s), :] = jnp.dot(ub, bre_ref[...], preferred_element_type=jnp.float32)
        sim_ref[stage_rows(s), :] = jnp.dot(ub, bim_ref[...], preferred_element_type=jnp.float32)

    def recur(s, hr, hi):
        for t in range(s * steps_per_stage, (s + 1) * steps_per_stage):
            rows = slice(t * bsz, (t + 1) * bsz)
            nhr = ar * hr - ai * hi + sre_ref[rows, :]
            nhi = ar * hi + ai * hr + sim_ref[rows, :]
            sre_ref[rows, :] = nhr
            sim_ref[rows, :] = nhi
            hr, hi = nhr, nhi
        return hr, hi

    def project_out(s):
        rows = stage_rows(s)
        y = jnp.dot(sre_ref[rows, :].astype(jnp.bfloat16), cre_ref[...], preferred_element_type=jnp.float32)
        y = y - jnp.dot(sim_ref[rows, :].astype(jnp.bfloat16), cim_ref[...], preferred_element_type=jnp.float32)
        y = y + d_ref[...] * u_ref[rows, :]
        g_ref[rows, :] = jax.nn.gelu(y).astype(jnp.bfloat16)

    hr, hi = hre_ref[...], him_ref[...]
    project_in(0)
    for s in range(n_stages):
        if s + 1 < n_stages:
            project_in(s + 1)
        hr, hi = recur(s, hr, hi)
        project_out(s)
    hre_ref[...] = hr
    him_ref[...] = hi

    @pl.when(tc == pl.num_programs(1) - 1)
    def _():
        hre_out_ref[...] = hr
        him_out_ref[...] = hi


def ssm_core(uz, bsz, bre, bim, cre, cim, ar, ai, d_skip, h0_re, h0_im, *, n_steps):
    rows = uz.shape[0]
    seq = rows // bsz
    n_steps = min(n_steps, seq)
    tr = n_steps * bsz
    kern = functools.partial(_ssm_kernel, n_steps=n_steps, bsz=bsz)
    f32 = jnp.float32
    wspec = lambda shape: pl.BlockSpec((None,) + shape, lambda gb, tc: (gb, 0, 0))
    return pl.pallas_call(
        kern,
        grid=(N_GROUP_BLOCKS, seq // n_steps),
        in_specs=[
            pl.BlockSpec((tr, BLOCK_CHANNELS), lambda gb, tc: (tc, gb)),
            wspec((BLOCK_CHANNELS, BLOCK_STATES)), wspec((BLOCK_CHANNELS, BLOCK_STATES)),
            wspec((BLOCK_STATES, BLOCK_CHANNELS)), wspec((BLOCK_STATES, BLOCK_CHANNELS)),
            wspec((1, BLOCK_STATES)), wspec((1, BLOCK_STATES)), wspec((1, BLOCK_CHANNELS)),
            pl.BlockSpec((bsz, BLOCK_STATES), lambda gb, tc: (0, gb)),
            pl.BlockSpec((bsz, BLOCK_STATES), lambda gb, tc: (0, gb)),
        ],
        out_specs=[
            pl.BlockSpec((tr, BLOCK_CHANNELS), lambda gb, tc: (tc, gb)),
            pl.BlockSpec((bsz, BLOCK_STATES), lambda gb, tc: (0, gb)),
            pl.BlockSpec((bsz, BLOCK_STATES), lambda gb, tc: (0, gb)),
        ],
        out_shape=[
            jax.ShapeDtypeStruct((rows, SSM_WIDTH), jnp.bfloat16),
            jax.ShapeDtypeStruct((bsz, SSM_GROUPS * SSM_STATE), f32),
            jax.ShapeDtypeStruct((bsz, SSM_GROUPS * SSM_STATE), f32),
        ],
        scratch_shapes=[pltpu.VMEM((tr, BLOCK_STATES), f32), pltpu.VMEM((tr, BLOCK_STATES), f32),
                        pltpu.VMEM((bsz, BLOCK_STATES), f32), pltpu.VMEM((bsz, BLOCK_STATES), f32)],
        compiler_params=_params("parallel", "arbitrary"),
    )(uz, bre, bim, cre, cim, ar, ai, d_skip, h0_re, h0_im)


def _glu_kernel(g_ref, wa_ref, wb_ref, ba_ref, bb_ref, z_ref, o_ref):
    g = g_ref[...]
    a = jnp.dot(g, wa_ref[...], preferred_element_type=jnp.float32) + ba_ref[...]
    b = jnp.dot(g, wb_ref[...], preferred_element_type=jnp.float32) + bb_ref[...]
    o_ref[...] = (a * jax.nn.sigmoid(b) * jax.nn.silu(z_ref[...])).astype(jnp.bfloat16)


def glu(g, w_glu, b_glu, uz, *, tn=512):
    rows, e = g.shape
    tm = min(ROW_TILE, rows)
    nj = e // tn
    return pl.pallas_call(
        _glu_kernel,
        grid=(nj, rows // tm),
        in_specs=[
            pl.BlockSpec((tm, e), lambda j, i: (i, 0)),
            pl.BlockSpec((e, tn), lambda j, i: (0, j)),
            pl.BlockSpec((e, tn), lambda j, i: (0, nj + j)),
            pl.BlockSpec((1, tn), lambda j, i: (0, j)),
            pl.BlockSpec((1, tn), lambda j, i: (0, nj + j)),
            pl.BlockSpec((tm, tn), lambda j, i: (i, nj + j)),
        ],
        out_specs=pl.BlockSpec((tm, tn), lambda j, i: (i, j)),
        out_shape=jax.ShapeDtypeStruct((rows, e), jnp.bfloat16),
        compiler_params=_params("parallel", "parallel"),
    )(g, w_glu, w_glu, b_glu, b_glu, uz)


def _out_proj_kernel(h_ref, w_ref, g_ref, x_ref, o_ref):
    y = jnp.dot(h_ref[...], w_ref[...], preferred_element_type=jnp.float32)
    o_ref[...] = x_ref[...] + y * _rms_scale(y) * g_ref[...]


def _gated_out_proj_kernel(a_ref, z_ref, w_ref, g_ref, x_ref, o_ref):
    h = (a_ref[...] * jax.nn.silu(z_ref[...])).astype(jnp.bfloat16)
    y = jnp.dot(h, w_ref[...], preferred_element_type=jnp.float32)
    o_ref[...] = x_ref[...] + y * _rms_scale(y) * g_ref[...]


def out_proj(h, w, g, x):
    rows, d = x.shape
    k = w.shape[0]
    tm = min(ROW_TILE, rows)
    return pl.pallas_call(
        _out_proj_kernel,
        grid=(rows // tm,),
        in_specs=[
            pl.BlockSpec((tm, k), lambda i: (i, 0)),
            pl.BlockSpec((k, d), lambda i: (0, 0)),
            pl.BlockSpec((1, d), lambda i: (0, 0)),
            pl.BlockSpec((tm, d), lambda i: (i, 0)),
        ],
        out_specs=pl.BlockSpec((tm, d), lambda i: (i, 0)),
        out_shape=jax.ShapeDtypeStruct((rows, d), jnp.float32),
        compiler_params=_params("parallel"),
    )(h, w, g.reshape(1, d), x)


def gated_out_proj(a, proj, z_block, w, g, x):
    rows, d = x.shape
    k = w.shape[0]
    return pl.pallas_call(
        _gated_out_proj_kernel,
        grid=(1,),
        in_specs=[
            pl.BlockSpec((rows, k), lambda i: (0, 0)),
            pl.BlockSpec((rows, k), lambda i: (0, z_block)),
            pl.BlockSpec((k, d), lambda i: (0, 0)),
            pl.BlockSpec((1, d), lambda i: (0, 0)),
            pl.BlockSpec((rows, d), lambda i: (0, 0)),
        ],
        out_specs=pl.BlockSpec((rows, d), lambda i: (0, 0)),
        out_shape=jax.ShapeDtypeStruct((rows, d), jnp.float32),
        compiler_params=_params("arbitrary"),
    )(a, proj, w, g.reshape(1, d), x)


QB = N_STRIDES
HEAD_PAIR = 2 * HEAD_DIM
LOG2_E = math.log2(math.e)
ATTN_UNROLL = 16


def _prompt_attn_kernel(slopes_ref, q0_ref, q1_ref, q2_ref, z_ref, k_ref, v_ref, o_ref,
                        acc0, acc1, acc2, m0, m1, m2, l0, l1, l2, va0_ref, va1_ref, bias_ref, *, seq):
    hp = pl.program_id(1)
    q_refs = (q0_ref, q1_ref, q2_ref)
    accs, ms, ls = (acc0, acc1, acc2), (m0, m1, m2), (l0, l1, l2)
    first_head = lax.broadcasted_iota(jnp.int32, (QB, HEAD_PAIR), 1) < HEAD_DIM
    nt = (((1,), (1,)), ((), ()))
    blocks_per_group = seq // QB
    bf16 = jnp.bfloat16
    q_scale = LOG2_E / math.sqrt(HEAD_DIM)

    def fill_values(i, _):
        rows = pl.ds(pl.multiple_of(i * QB, QB), QB)
        v = v_ref[rows, :]
        va0_ref[rows, :] = jnp.where(first_head, v, 1.0)
        va1_ref[rows, :] = jnp.where(first_head, 1.0, v)
        return 0

    lax.fori_loop(0, seq // QB, fill_values, 0, unroll=2)

    for g, dil in enumerate(DILATIONS):
        blocks_per_res = seq // dil // QB
        has_prev_block = blocks_per_res > 1
        n_keys = 2 * QB if has_prev_block else QB

        qi = lax.broadcasted_iota(jnp.int32, (QB, n_keys), 0)
        kj = lax.broadcasted_iota(jnp.int32, (QB, n_keys), 1)
        dist = qi + (n_keys - QB) - kj
        valid = jnp.logical_and(dist >= 0, dist <= N_STRIDES)
        distf = dist.astype(jnp.float32)
        for hh in range(2):
            slope = slopes_ref[g * N_HEADS + 2 * hp + hh] * (float(dil) * LOG2_E)
            table = jnp.where(valid, -slope * distf, NEG_INF)
            bias_ref[1, hh * QB:(hh + 1) * QB, 0:n_keys] = table
            if has_prev_block:
                bias_ref[0, hh * QB:(hh + 1) * QB, 0:n_keys] = jnp.where(kj >= QB, table, NEG_INF)

        def body(j, _, g=g, dil=dil, has_prev_block=has_prev_block, n_keys=n_keys):
            if dil == 1:
                start = pl.multiple_of(j * QB, QB)
                prev_start = jnp.maximum(start - QB, 0)
                rows = lambda s: pl.ds(s, QB)
                has_prev = jnp.minimum(j, 1)
            else:
                blk = j // dil
                res = j - blk * dil
                start = res + blk * (QB * dil)
                prev_start = jnp.maximum(start - QB * dil, 0)
                rows = lambda s: pl.ds(s, QB, stride=dil)
                has_prev = jnp.minimum(blk, 1)
            q2 = q_refs[g][rows(start), :] * q_scale
            qs = jnp.concatenate([jnp.where(first_head, q2, 0.0), jnp.where(first_head, 0.0, q2)], axis=0)
            if has_prev_block:
                kk = jnp.concatenate([k_ref[rows(prev_start), :], k_ref[rows(start), :]], axis=0)
                v_aug = jnp.concatenate(
                    [jnp.concatenate([va0_ref[rows(prev_start), :], va0_ref[rows(start), :]], axis=0),
                     jnp.concatenate([va1_ref[rows(prev_start), :], va1_ref[rows(start), :]], axis=0)], axis=1)
                bias = bias_ref[has_prev]
            else:
                kk = k_ref[rows(start), :]
                v_aug = jnp.concatenate([va0_ref[rows(start), :], va1_ref[rows(start), :]], axis=1)
                bias = bias_ref[1, :, 0:n_keys]
            s = lax.dot_general(qs.astype(bf16), kk.astype(bf16), nt, preferred_element_type=jnp.float32)
            s = s + bias
            mx = jnp.max(s, axis=-1, keepdims=True)
            p = jnp.exp2(s - mx).astype(bf16)
            r = jnp.dot(p, v_aug.astype(bf16), preferred_element_type=jnp.float32)
            r0 = r[:QB, :HEAD_PAIR]
            r1 = r[QB:, HEAD_PAIR:]
            accs[g][rows(start), :] = jnp.where(first_head, r0, r1)
            ls[g][rows(start), :] = pltpu.roll(jnp.where(first_head, r1, r0), HEAD_DIM, 1)
            ms[g][rows(start), :] = jnp.where(first_head, jnp.broadcast_to(mx[:QB], (QB, HEAD_PAIR)),
                                              jnp.broadcast_to(mx[QB:], (QB, HEAD_PAIR)))
            return 0

        lax.fori_loop(0, blocks_per_group, body, 0, unroll=ATTN_UNROLL)

    def merge(i, _):
        rows = pl.ds(pl.multiple_of(i * QB, QB), QB)
        mm = jnp.maximum(jnp.maximum(m0[rows, :], m1[rows, :]), m2[rows, :])
        num = jnp.zeros((QB, HEAD_PAIR), jnp.float32)
        den = jnp.zeros((QB, HEAD_PAIR), jnp.float32)
        for g in range(N_GROUPS):
            e = jnp.exp2(ms[g][rows, :] - mm)
            num = num + e * accs[g][rows, :]
            den = den + e * ls[g][rows, :]
        o_ref[rows, :] = ((num / den) * jax.nn.silu(z_ref[rows, :])).astype(jnp.bfloat16)
        return 0

    lax.fori_loop(0, seq // QB, merge, 0, unroll=2)


def prompt_attention(slopes, proj, kv):
    bsz, seq, _ = proj.shape
    n_pairs = ATTN_WIDTH // HEAD_PAIR
    col = lambda off: pl.BlockSpec((None, seq, HEAD_PAIR), lambda b, hp, off=off: (b, 0, off + hp))
    f32 = jnp.float32
    scratch = [pltpu.VMEM((seq, HEAD_PAIR), f32) for _ in range(11)]
    scratch.append(pltpu.VMEM((2, 2 * QB, 2 * QB), f32))
    return pl.pallas_call(
        functools.partial(_prompt_attn_kernel, seq=seq),
        grid=(bsz, n_pairs),
        in_specs=[pl.BlockSpec(memory_space=pltpu.SMEM),
                  col(0), col(n_pairs), col(2 * n_pairs), col(3 * n_pairs), col(0), col(n_pairs)],
        out_specs=pl.BlockSpec((None, seq, HEAD_PAIR), lambda b, hp: (b, 0, hp)),
        out_shape=jax.ShapeDtypeStruct((bsz, seq, ATTN_WIDTH), jnp.bfloat16),
        scratch_shapes=scratch,
        compiler_params=_params("parallel", "parallel"),
    )(slopes, proj, proj, proj, proj, kv, kv)


def _select_positions(x, lo, dil):
    width = x.shape[1] - lo
    pos = lax.broadcasted_iota(jnp.int32, (width, LANES), 0)
    col = lax.broadcasted_iota(jnp.int32, (width, LANES), 1)
    sel = jnp.where(pos == col * dil, 1.0, 0.0).astype(jnp.bfloat16)
    picked = jnp.dot(x[:, lo:].astype(jnp.bfloat16), sel, preferred_element_type=jnp.float32)
    return picked.astype(jnp.bfloat16)


def _cache_shift_kernel(c_ref, new_ref, o_ref, *win_refs):
    b = pl.program_id(0)
    rows, n_pos = c_ref.shape
    x = c_ref[...]
    o_ref[...] = pltpu.roll(x, n_pos - 1, 1)
    new = new_ref[...]
    lane = lax.broadcasted_iota(jnp.int32, new.shape, 1)
    new_col = jnp.sum(jnp.where(lane == b, new, 0.0), axis=1, keepdims=True)
    last = o_ref[:, n_pos - LANES:]
    last_lane = lax.broadcasted_iota(jnp.int32, last.shape, 1) == LANES - 1
    o_ref[:, n_pos - LANES:] = jnp.where(last_lane, new_col, last)
    for ref, dil in zip(win_refs, DILATIONS):
        if dil == 1:
            ref[...] = x[:, n_pos - LANES:].astype(jnp.bfloat16)
        else:
            ref[...] = _select_positions(x, n_pos - N_STRIDES * dil, dil)


def cache_shift(cache_t, new_t, *, rows=512):
    bsz, chans, n_pos = cache_t.shape
    n_win = len(DILATIONS)
    win_spec = pl.BlockSpec((None, rows, LANES), lambda b, i: (b, i, 0))
    win_shape = jax.ShapeDtypeStruct((bsz, chans, LANES), jnp.bfloat16)
    return pl.pallas_call(
        _cache_shift_kernel,
        grid=(bsz, chans // rows),
        in_specs=[pl.BlockSpec((None, rows, n_pos), lambda b, i: (b, i, 0)),
                  pl.BlockSpec((rows, bsz), lambda b, i: (i, 0))],
        out_specs=[pl.BlockSpec((None, rows, n_pos), lambda b, i: (b, i, 0))] + [win_spec] * n_win,
        out_shape=[jax.ShapeDtypeStruct(cache_t.shape, cache_t.dtype)] + [win_shape] * n_win,
        compiler_params=_params("parallel", "parallel"),
    )(cache_t, new_t)


SEQS_PER_STEP = 8


def _sample_attn_kernel(bias_ref, proj_ref, new_ref, c1_ref, c4_ref, c16_ref, o_ref):
    packed = (c1_ref, c4_ref, c16_ref)
    head_of_chan = lax.broadcasted_iota(jnp.int32, (N_HEADS, ATTN_WIDTH), 1) // HEAD_DIM
    own = head_of_chan == lax.broadcasted_iota(jnp.int32, (N_HEADS, ATTN_WIDTH), 0)
    nt = (((1,), (1,)), ((), ()))
    bf16 = jnp.bfloat16

    def one_sequence(i, _):
        proj = proj_ref[i]
        new = new_ref[i]
        k_new = new[:, :ATTN_WIDTH]
        v_new = new[:, ATTN_WIDTH:]
        scores, new_scores = [], []
        for g in range(N_GROUPS):
            q = proj[:, g * ATTN_WIDTH:(g + 1) * ATTN_WIDTH] * (1.0 / math.sqrt(HEAD_DIM))
            q_bd = jnp.where(own, q, 0.0)
            k = packed[g][i, 0:ATTN_WIDTH, :]
            s = jnp.dot(q_bd.astype(bf16), k, preferred_element_type=jnp.float32) + bias_ref[g]
            scores.append(s)
            new_scores.append(jnp.sum(q_bd * k_new, axis=1, keepdims=True))
        mx = jnp.maximum(jnp.maximum(new_scores[0], new_scores[1]), new_scores[2])
        for g in range(N_GROUPS):
            mx = jnp.maximum(mx, jnp.max(scores[g], axis=1, keepdims=True))
        p_new = sum(jnp.exp(s - mx) for s in new_scores)
        den = p_new
        o = p_new * v_new
        for g in range(N_GROUPS):
            p = jnp.exp(scores[g] - mx)
            den = den + jnp.sum(p, axis=1, keepdims=True)
            v = packed[g][i, ATTN_WIDTH:2 * ATTN_WIDTH, :]
            o = o + lax.dot_general(p.astype(bf16), v, nt, preferred_element_type=jnp.float32)
        o_ref[i] = jnp.sum(jnp.where(own, o / den, 0.0), axis=0, keepdims=True)
        return 0

    lax.fori_loop(0, proj_ref.shape[0], one_sequence, 0)


def sample_attention(slopes, proj, kv_new, windows):
    bsz, chans, _ = windows[0].shape
    nb = SEQS_PER_STEP
    steps = (N_STRIDES - jnp.arange(LANES)).astype(jnp.float32)
    dil = jnp.asarray(DILATIONS, jnp.float32)
    bias = -(slopes.reshape(N_GROUPS, N_HEADS) * dil[:, None])[:, :, None] * steps
    row3 = lambda width: pl.BlockSpec((nb, 1, width), lambda b: (b, 0, 0))
    win = pl.BlockSpec((nb, chans, LANES), lambda b: (b, 0, 0))
    out = pl.pallas_call(
        _sample_attn_kernel,
        grid=(bsz // nb,),
        in_specs=[pl.BlockSpec((N_GROUPS, N_HEADS, LANES), lambda b: (0, 0, 0)),
                  row3(proj.shape[1]), row3(chans), win, win, win],
        out_specs=row3(ATTN_WIDTH),
        out_shape=jax.ShapeDtypeStruct((bsz, 1, ATTN_WIDTH), jnp.float32),
        compiler_params=_params("parallel"),
    )(bias, proj.reshape(bsz, 1, -1), kv_new.reshape(bsz, 1, chans), *windows)
    return out.reshape(bsz, ATTN_WIDTH)


def _alibi_slopes():
    n = N_GROUPS * N_HEADS
    e = jnp.arange(1, n + 1, dtype=jnp.float32)
    return jnp.exp2(-ALIBI_MAX_EXP * e / n)


def _ssm_weights(lam_re, lam_im, log_dt, b_re, b_im, c_re, c_im, d_skip):
    ab_re, ab_im, bbt_re, bbt_im = discretize(lam_re, lam_im, log_dt, b_re, b_im)
    bf16 = jnp.bfloat16
    return dict(
        bre=_block_diag(bbt_re, True).astype(bf16), bim=_block_diag(bbt_im, True).astype(bf16),
        cre=_block_diag(c_re, False).astype(bf16), cim=_block_diag(c_im, False).astype(bf16),
        ar=ab_re.reshape(N_GROUP_BLOCKS, 1, BLOCK_STATES), ai=ab_im.reshape(N_GROUP_BLOCKS, 1, BLOCK_STATES),
        d_skip=d_skip.reshape(N_GROUP_BLOCKS, 1, BLOCK_CHANNELS))


def _ssm_layer(x, h0_re, h0_im, norm_pre, w_in, ssm_w, w_glu, b_glu, w_out, norm_post, *, n_steps):
    uz = norm_matmul(x, norm_pre, w_in)
    g, h_re, h_im = ssm_core(uz, h0_re.shape[0], h0_re=h0_re, h0_im=h0_im, n_steps=n_steps, **ssm_w)
    h2 = glu(g, w_glu, b_glu.reshape(1, -1), uz)
    return out_proj(h2, w_out, norm_post, x), h_re, h_im


def _states_out(h, bsz):
    return h.reshape(bsz, SSM_GROUPS, SSM_STATE)


def kernel(x_prompt, x_sample, state_s5_re, state_s5_im, cache_kv, a_norm_pre, a_w_in, a_lam_re, a_lam_im, a_log_dt, a_b_re, a_b_im, a_c_re, a_c_im, a_d, a_w_glu, a_b_glu, a_w_out, a_norm_post, kv_norm, w_kv, b_norm_pre, b_w_in, b_w_out, b_norm_post):
    bf16 = jnp.bfloat16
    n_a = a_w_in.shape[0]
    n_b = b_w_in.shape[0]
    bsz, seq, _ = x_prompt.shape
    dbsz = x_sample.shape[0]
    n_pos = cache_kv.shape[1]

    a_w_in_b = a_w_in.astype(bf16)
    a_w_glu_b = a_w_glu.astype(bf16)
    a_w_out_b = a_w_out.astype(bf16)
    w_kv_b = w_kv.astype(bf16)
    w_kv_t = w_kv_b.T
    b_w_in_b = b_w_in.astype(bf16)
    b_w_out_b = b_w_out.astype(bf16)
    slopes = _alibi_slopes()
    ssm_ws = [_ssm_weights(a_lam_re[l], a_lam_im[l], a_log_dt[l], a_b_re[l], a_b_im[l],
                           a_c_re[l], a_c_im[l], a_d[l]) for l in range(n_a)]

    x = jnp.swapaxes(x_prompt, 0, 1).reshape(seq * bsz, D_MODEL)
    zeros = jnp.zeros((bsz, SSM_GROUPS * SSM_STATE), jnp.float32)
    p_re, p_im = [], []
    for l in range(n_a):
        x, h_re, h_im = _ssm_layer(x, zeros, zeros, a_norm_pre[l], a_w_in_b[l], ssm_ws[l], a_w_glu_b[l],
                                   a_b_glu[l], a_w_out_b[l], a_norm_post[l], n_steps=64)
        p_re.append(_states_out(h_re, bsz))
        p_im.append(_states_out(h_im, bsz))
    x = jnp.swapaxes(x.reshape(seq, bsz, D_MODEL), 0, 1)
    kv, kv_t = kv_project(x, kv_norm, w_kv_b, w_kv_t)
    x = x.reshape(bsz * seq, D_MODEL)
    for l in range(n_b):
        proj = norm_matmul(x, b_norm_pre[l], b_w_in_b[l])
        o = prompt_attention(slopes, proj.reshape(bsz, seq, -1), kv)
        x = out_proj(o.reshape(bsz * seq, ATTN_WIDTH), b_w_out_b[l], b_norm_post[l], x)
    y_prompt = x.reshape(bsz, seq, D_MODEL)
    prompt_kv = jnp.transpose(kv_t.reshape(bsz, 2, N_HEADS, HEAD_DIM, seq), (0, 4, 1, 2, 3))

    xs = x_sample.reshape(dbsz, D_MODEL)
    s_re, s_im = [], []
    for l in range(n_a):
        h0_re = state_s5_re[l].reshape(dbsz, SSM_GROUPS * SSM_STATE)
        h0_im = state_s5_im[l].reshape(dbsz, SSM_GROUPS * SSM_STATE)
        xs, h_re, h_im = _ssm_layer(xs, h0_re, h0_im, a_norm_pre[l], a_w_in_b[l], ssm_ws[l], a_w_glu_b[l],
                                    a_b_glu[l], a_w_out_b[l], a_norm_post[l], n_steps=1)
        s_re.append(_states_out(h_re, dbsz))
        s_im.append(_states_out(h_im, dbsz))
    kv_s, kv_s_t = kv_project(xs[None], kv_norm, w_kv_b, w_kv_t)
    cache_t = jnp.transpose(cache_kv, (0, 2, 3, 4, 1)).reshape(dbsz, 2 * ATTN_WIDTH, n_pos)
    new_cache_t, *windows = cache_shift(cache_t, kv_s_t[0])
    sample_kv = jnp.transpose(new_cache_t.reshape(dbsz, 2, N_HEADS, HEAD_DIM, n_pos), (0, 4, 1, 2, 3))

    for l in range(n_b):
        proj = norm_matmul(xs, b_norm_pre[l], b_w_in_b[l])
        o = sample_attention(slopes, proj, kv_s[0], windows)
        xs = gated_out_proj(o, proj, N_GROUPS, b_w_out_b[l], b_norm_post[l], xs)
    y_sample = xs.reshape(dbsz, 1, D_MODEL)

    return (y_prompt, y_sample, jnp.stack(p_re), jnp.stack(p_im), prompt_kv,
            jnp.stack(s_re), jnp.stack(s_im), sample_kv)
```

```python
import functools
import math

import jax
import jax.numpy as jnp
from jax import lax
from jax.experimental import pallas as pl
from jax.experimental.pallas import tpu as pltpu

D_MODEL = 1024
SSM_WIDTH = 2048
SSM_GROUP = 16
SSM_GROUPS = SSM_WIDTH // SSM_GROUP
SSM_STATE = 64
HEAD_DIM = 64
N_HEADS = 16
ATTN_WIDTH = N_HEADS * HEAD_DIM
DILATIONS = (1, 4, 16)
N_STRIDES = 128
N_GROUPS = len(DILATIONS)
ALIBI_MAX_EXP = 8.0
RMS_EPS = 1e-6
NEG_INF = -1e30

VMEM_LIMIT_BYTES_V7X = 56 * 1024 * 1024
LANES = 128

GROUPS_PER_BLOCK = 16
N_GROUP_BLOCKS = SSM_GROUPS // GROUPS_PER_BLOCK
BLOCK_CHANNELS = GROUPS_PER_BLOCK * SSM_GROUP
BLOCK_STATES = GROUPS_PER_BLOCK * SSM_STATE
SSM_STAGES = 2

ROW_TILE = 512


def _params(*semantics):
    return pltpu.CompilerParams(dimension_semantics=semantics,
                                vmem_limit_bytes=VMEM_LIMIT_BYTES_V7X)


def _rms_scale(x):
    return lax.rsqrt(jnp.mean(x * x, axis=-1, keepdims=True) + RMS_EPS)


def _norm_matmul_kernel(x_ref, g_ref, w_ref, o_ref):
    x = x_ref[...]
    xn = (x * _rms_scale(x) * g_ref[...]).astype(jnp.bfloat16)
    o_ref[...] = jnp.dot(xn, w_ref[...], preferred_element_type=jnp.float32)


def norm_matmul(x, g, w, layer):
    rows, d = x.shape
    n = w.shape[2]
    tm = min(ROW_TILE, rows)
    return pl.pallas_call(
        _norm_matmul_kernel,
        grid=(rows // tm,),
        in_specs=[
            pl.BlockSpec((tm, d), lambda i: (i, 0)),
            pl.BlockSpec((1, d), lambda i: (0, 0)),
            pl.BlockSpec((None, d, n), lambda i: (layer, 0, 0)),
        ],
        out_specs=pl.BlockSpec((tm, n), lambda i: (i, 0)),
        out_shape=jax.ShapeDtypeStruct((rows, n), jnp.float32),
        compiler_params=_params("parallel"),
    )(x, g.reshape(1, d), w)


def _kv_kernel(x_ref, g_ref, w_ref, kv_ref, kvt_ref):
    x = x_ref[...]
    xn = (x * _rms_scale(x) * g_ref[...]).astype(jnp.bfloat16)
    kv = jnp.dot(xn, w_ref[...], preferred_element_type=jnp.float32)
    kv_ref[...] = kv
    kvt_ref[...] = kv.T


def kv_project(x, g, w, *, tm=256):
    bsz, seq, d = x.shape
    n = w.shape[1]
    tm = min(tm, seq)
    return pl.pallas_call(
        _kv_kernel,
        grid=(bsz, seq // tm),
        in_specs=[
            pl.BlockSpec((None, tm, d), lambda b, i: (b, i, 0)),
            pl.BlockSpec((1, d), lambda b, i: (0, 0)),
            pl.BlockSpec((d, n), lambda b, i: (0, 0)),
        ],
        out_specs=[
            pl.BlockSpec((None, tm, n), lambda b, i: (b, i, 0)),
            pl.BlockSpec((None, n, tm), lambda b, i: (b, 0, i)),
        ],
        out_shape=[
            jax.ShapeDtypeStruct((bsz, seq, n), jnp.float32),
            jax.ShapeDtypeStruct((bsz, n, seq), jnp.float32),
        ],
        compiler_params=_params("parallel", "parallel"),
    )(x, g.reshape(1, d), w)


def _ssm_weights_kernel(lam_re_ref, lam_im_ref, log_dt_ref, bt_re_ref, bt_im_ref, ct_re_ref, ct_im_ref,
                        ab_re_ref, ab_im_ref, bre_ref, bim_ref, cre_ref, cim_ref):
    lam_re = lam_re_ref[...]
    lam_im = lam_im_ref[...]
    dt = jnp.exp(log_dt_ref[...])
    mag = jnp.exp(lam_re * dt)
    ab_re = mag * jnp.cos(lam_im * dt)
    ab_im = mag * jnp.sin(lam_im * dt)
    den = lam_re * lam_re + lam_im * lam_im
    num_re = ab_re - 1.0
    coef_re = (num_re * lam_re + ab_im * lam_im) / den
    coef_im = (ab_im * lam_re - num_re * lam_im) / den
    ab_re_ref[...] = ab_re
    ab_im_ref[...] = ab_im
    b_re = bt_re_ref[...]
    b_im = bt_im_ref[...]
    cr = coef_re[:, None, :]
    ci = coef_im[:, None, :]
    bbt_re = cr * b_re - ci * b_im
    bbt_im = cr * b_im + ci * b_re

    bf16 = jnp.bfloat16

    def block_diag(w, rows, cols, rows_per_group, cols_per_group):
        src = lax.broadcasted_iota(jnp.int32, (cols_per_group, cols), 0)
        dst = lax.broadcasted_iota(jnp.int32, (cols_per_group, cols), 1)
        spread = jnp.where((dst & (cols_per_group - 1)) == src, 1.0, 0.0).astype(bf16)
        tiled = jnp.dot(w.astype(bf16), spread, preferred_element_type=jnp.float32)
        r = lax.broadcasted_iota(jnp.int32, (rows, cols), 0) // rows_per_group
        c = lax.broadcasted_iota(jnp.int32, (rows, cols), 1) // cols_per_group
        return jnp.where(r == c, tiled, 0.0).astype(bf16)

    gc, p = b_re.shape[1], b_re.shape[2]
    bre_ref[...] = block_diag(bbt_re.reshape(BLOCK_CHANNELS, p), BLOCK_CHANNELS, BLOCK_STATES, gc, p)
    bim_ref[...] = block_diag(bbt_im.reshape(BLOCK_CHANNELS, p), BLOCK_CHANNELS, BLOCK_STATES, gc, p)
    cre_ref[...] = block_diag(ct_re_ref[...].reshape(BLOCK_STATES, gc), BLOCK_STATES, BLOCK_CHANNELS, p, gc)
    cim_ref[...] = block_diag(ct_im_ref[...].reshape(BLOCK_STATES, gc), BLOCK_STATES, BLOCK_CHANNELS, p, gc)


def ssm_weights(lam_re, lam_im, log_dt, b_re, b_im, c_re, c_im, d_skip):
    g, p = lam_re.shape
    gc = b_re.shape[-1]
    gpb = GROUPS_PER_BLOCK
    f32, bf16 = jnp.float32, jnp.bfloat16
    grp2 = lambda w: pl.BlockSpec((gpb, w), lambda i: (i, 0))
    grp3 = lambda a, b: pl.BlockSpec((gpb, a, b), lambda i: (i, 0, 0))
    blk = lambda a, b: pl.BlockSpec((None, a, b), lambda i: (i, 0, 0))
    ab_re, ab_im, bre, bim, cre, cim = pl.pallas_call(
        _ssm_weights_kernel,
        grid=(N_GROUP_BLOCKS,),
        in_specs=[grp2(p), grp2(p), grp2(1), grp3(gc, p), grp3(gc, p), grp3(p, gc), grp3(p, gc)],
        out_specs=[grp2(p), grp2(p), blk(BLOCK_CHANNELS, BLOCK_STATES), blk(BLOCK_CHANNELS, BLOCK_STATES),
                   blk(BLOCK_STATES, BLOCK_CHANNELS), blk(BLOCK_STATES, BLOCK_CHANNELS)],
        out_shape=[jax.ShapeDtypeStruct((g, p), f32), jax.ShapeDtypeStruct((g, p), f32),
                   jax.ShapeDtypeStruct((N_GROUP_BLOCKS, BLOCK_CHANNELS, BLOCK_STATES), bf16),
                   jax.ShapeDtypeStruct((N_GROUP_BLOCKS, BLOCK_CHANNELS, BLOCK_STATES), bf16),
                   jax.ShapeDtypeStruct((N_GROUP_BLOCKS, BLOCK_STATES, BLOCK_CHANNELS), bf16),
                   jax.ShapeDtypeStruct((N_GROUP_BLOCKS, BLOCK_STATES, BLOCK_CHANNELS), bf16)],
        compiler_params=_params("parallel"),
    )(lam_re, lam_im, log_dt.reshape(g, 1), jnp.swapaxes(b_re, 1, 2), jnp.swapaxes(b_im, 1, 2),
      jnp.swapaxes(c_re, 1, 2), jnp.swapaxes(c_im, 1, 2))
    return dict(bre=bre, bim=bim, cre=cre, cim=cim,
                ar=ab_re.reshape(N_GROUP_BLOCKS, 1, BLOCK_STATES), ai=ab_im.reshape(N_GROUP_BLOCKS, 1, BLOCK_STATES),
                d_skip=d_skip.reshape(N_GROUP_BLOCKS, 1, BLOCK_CHANNELS))


def _ssm_kernel(u_ref, bre_ref, bim_ref, cre_ref, cim_ref, ar_ref, ai_ref, d_ref,
                h0re_ref, h0im_ref, g_ref, hre_out_ref, him_out_ref,
                sre_ref, sim_ref, hre_ref, him_ref, *, n_steps, bsz):
    tc = pl.program_id(1)

    @pl.when(tc == 0)
    def _():
        hre_ref[...] = h0re_ref[...]
        him_ref[...] = h0im_ref[...]

    ar = jnp.broadcast_to(ar_ref[...], (bsz, BLOCK_STATES))
    ai = jnp.broadcast_to(ai_ref[...], (bsz, BLOCK_STATES))

    n_stages = min(SSM_STAGES, n_steps)
    steps_per_stage = n_steps // n_stages
    stage_rows = lambda s: slice(s * steps_per_stage * bsz, (s + 1) * steps_per_stage * bsz)

    def project_in(s):
        ub = u_ref[stage_rows(s), :].astype(jnp.bfloat16)
        sre_ref[stage_rows(s), :] = jnp.dot(ub, bre_ref[...], preferred_element_type=jnp.float32)
        sim_ref[stage_rows(s), :] = jnp.dot(ub, bim_ref[...], preferred_element_type=jnp.float32)

    def recur(s, hr, hi):
        for t in range(s * steps_per_stage, (s + 1) * steps_per_stage):
            rows = slice(t * bsz, (t + 1) * bsz)
            nhr = ar * hr - ai * hi + sre_ref[rows, :]
            nhi = ar * hi + ai * hr + sim_ref[rows, :]
            sre_ref[rows, :] = nhr
            sim_ref[rows, :] = nhi
            hr, hi = nhr, nhi
        return hr, hi

    def project_out(s):
        rows = stage_rows(s)
        y = jnp.dot(sre_ref[rows, :].astype(jnp.bfloat16), cre_ref[...], preferred_element_type=jnp.float32)
        y = y - jnp.dot(sim_ref[rows, :].astype(jnp.bfloat16), cim_ref[...], preferred_element_type=jnp.float32)
        y = y + d_ref[...] * u_ref[rows, :]
        g_ref[rows, :] = jax.nn.gelu(y).astype(jnp.bfloat16)

    hr, hi = hre_ref[...], him_ref[...]
    project_in(0)
    for s in range(n_stages):
        if s + 1 < n_stages:
            project_in(s + 1)
        hr, hi = recur(s, hr, hi)
        project_out(s)
    hre_ref[...] = hr
    him_ref[...] = hi

    @pl.when(tc == pl.num_programs(1) - 1)
    def _():
        hre_out_ref[...] = hr
        him_out_ref[...] = hi


def ssm_core(uz, bsz, bre, bim, cre, cim, ar, ai, d_skip, h0_re, h0_im, *, n_steps):
    rows = uz.shape[0]
    seq = rows // bsz
    n_steps = min(n_steps, seq)
    tr = n_steps * bsz
    kern = functools.partial(_ssm_kernel, n_steps=n_steps, bsz=bsz)
    f32 = jnp.float32
    wspec = lambda shape: pl.BlockSpec((None,) + shape, lambda gb, tc: (gb, 0, 0))
    return pl.pallas_call(
        kern,
        grid=(N_GROUP_BLOCKS, seq // n_steps),
        in_specs=[
            pl.BlockSpec((tr, BLOCK_CHANNELS), lambda gb, tc: (tc, gb)),
            wspec((BLOCK_CHANNELS, BLOCK_STATES)), wspec((BLOCK_CHANNELS, BLOCK_STATES)),
            wspec((BLOCK_STATES, BLOCK_CHANNELS)), wspec((BLOCK_STATES, BLOCK_CHANNELS)),
            wspec((1, BLOCK_STATES)), wspec((1, BLOCK_STATES)), wspec((1, BLOCK_CHANNELS)),
            pl.BlockSpec((bsz, BLOCK_STATES), lambda gb, tc: (0, gb)),
            pl.BlockSpec((bsz, BLOCK_STATES), lambda gb, tc: (0, gb)),
        ],
        out_specs=[
            pl.BlockSpec((tr, BLOCK_CHANNELS), lambda gb, tc: (tc, gb)),
            pl.BlockSpec((bsz, BLOCK_STATES), lambda gb, tc: (0, gb)),
            pl.BlockSpec((bsz, BLOCK_STATES), lambda gb, tc: (0, gb)),
        ],
        out_shape=[
            jax.ShapeDtypeStruct((rows, SSM_WIDTH), jnp.bfloat16),
            jax.ShapeDtypeStruct((bsz, SSM_GROUPS * SSM_STATE), f32),
            jax.ShapeDtypeStruct((bsz, SSM_GROUPS * SSM_STATE), f32),
        ],
        scratch_shapes=[pltpu.VMEM((tr, BLOCK_STATES), f32), pltpu.VMEM((tr, BLOCK_STATES), f32),
                        pltpu.VMEM((bsz, BLOCK_STATES), f32), pltpu.VMEM((bsz, BLOCK_STATES), f32)],
        compiler_params=_params("parallel", "arbitrary"),
    )(uz, bre, bim, cre, cim, ar, ai, d_skip, h0_re, h0_im)


def _glu_kernel(g_ref, wa_ref, wb_ref, ba_ref, bb_ref, z_ref, o_ref):
    g = g_ref[...]
    a = jnp.dot(g, wa_ref[...], preferred_element_type=jnp.float32) + ba_ref[...]
    b = jnp.dot(g, wb_ref[...], preferred_element_type=jnp.float32) + bb_ref[...]
    o_ref[...] = (a * jax.nn.sigmoid(b) * jax.nn.silu(z_ref[...])).astype(jnp.bfloat16)


def glu(g, w_glu, layer, b_glu, uz, *, tn=512):
    rows, e = g.shape
    tm = min(ROW_TILE, rows)
    nj = e // tn
    return pl.pallas_call(
        _glu_kernel,
        grid=(nj, rows // tm),
        in_specs=[
            pl.BlockSpec((tm, e), lambda j, i: (i, 0)),
            pl.BlockSpec((None, e, tn), lambda j, i: (layer, 0, j)),
            pl.BlockSpec((None, e, tn), lambda j, i: (layer, 0, nj + j)),
            pl.BlockSpec((1, tn), lambda j, i: (0, j)),
            pl.BlockSpec((1, tn), lambda j, i: (0, nj + j)),
            pl.BlockSpec((tm, tn), lambda j, i: (i, nj + j)),
        ],
        out_specs=pl.BlockSpec((tm, tn), lambda j, i: (i, j)),
        out_shape=jax.ShapeDtypeStruct((rows, e), jnp.bfloat16),
        compiler_params=_params("parallel", "parallel"),
    )(g, w_glu, w_glu, b_glu, b_glu, uz)


def _out_proj_kernel(h_ref, w_ref, g_ref, x_ref, o_ref):
    y = jnp.dot(h_ref[...], w_ref[...], preferred_element_type=jnp.float32)
    o_ref[...] = x_ref[...] + y * _rms_scale(y) * g_ref[...]


def _gated_out_proj_kernel(a_ref, z_ref, w_ref, g_ref, x_ref, o_ref):
    h = (a_ref[...] * jax.nn.silu(z_ref[...])).astype(jnp.bfloat16)
    y = jnp.dot(h, w_ref[...], preferred_element_type=jnp.float32)
    o_ref[...] = x_ref[...] + y * _rms_scale(y) * g_ref[...]


def out_proj(h, w, layer, g, x):
    rows, d = x.shape
    k = w.shape[1]
    tm = min(ROW_TILE, rows)
    return pl.pallas_call(
        _out_proj_kernel,
        grid=(rows // tm,),
        in_specs=[
            pl.BlockSpec((tm, k), lambda i: (i, 0)),
            pl.BlockSpec((None, k, d), lambda i: (layer, 0, 0)),
            pl.BlockSpec((1, d), lambda i: (0, 0)),
            pl.BlockSpec((tm, d), lambda i: (i, 0)),
        ],
        out_specs=pl.BlockSpec((tm, d), lambda i: (i, 0)),
        out_shape=jax.ShapeDtypeStruct((rows, d), jnp.float32),
        compiler_params=_params("parallel"),
    )(h, w, g.reshape(1, d), x)


def gated_out_proj(a, proj, z_block, w, layer, g, x):
    rows, d = x.shape
    k = w.shape[1]
    return pl.pallas_call(
        _gated_out_proj_kernel,
        grid=(1,),
        in_specs=[
            pl.BlockSpec((rows, k), lambda i: (0, 0)),
            pl.BlockSpec((rows, k), lambda i: (0, z_block)),
            pl.BlockSpec((None, k, d), lambda i: (layer, 0, 0)),
            pl.BlockSpec((1, d), lambda i: (0, 0)),
            pl.BlockSpec((rows, d), lambda i: (0, 0)),
        ],
        out_specs=pl.BlockSpec((rows, d), lambda i: (0, 0)),
        out_shape=jax.ShapeDtypeStruct((rows, d), jnp.float32),
        compiler_params=_params("arbitrary"),
    )(a, proj, w, g.reshape(1, d), x)


QB = N_STRIDES
HEAD_PAIR = 2 * HEAD_DIM
LOG2_E = math.log2(math.e)
ATTN_UNROLL = 16


def _prompt_attn_kernel(slopes_ref, q0_ref, q1_ref, q2_ref, z_ref, k_ref, v_ref, o_ref,
                        acc0, acc1, acc2, m0, m1, m2, l0, l1, l2, va0_ref, va1_ref, bias_ref, *, seq):
    hp = pl.program_id(1)
    q_refs = (q0_ref, q1_ref, q2_ref)
    accs, ms, ls = (acc0, acc1, acc2), (m0, m1, m2), (l0, l1, l2)
    first_head = lax.broadcasted_iota(jnp.int32, (QB, HEAD_PAIR), 1) < HEAD_DIM
    nt = (((1,), (1,)), ((), ()))
    blocks_per_group = seq // QB
    bf16 = jnp.bfloat16
    q_scale = LOG2_E / math.sqrt(HEAD_DIM)

    def fill_values(i, _):
        rows = pl.ds(pl.multiple_of(i * QB, QB), QB)
        v = v_ref[rows, :]
        va0_ref[rows, :] = jnp.where(first_head, v, 1.0)
        va1_ref[rows, :] = jnp.where(first_head, 1.0, v)
        return 0

    lax.fori_loop(0, seq // QB, fill_values, 0, unroll=2)

    for g, dil in enumerate(DILATIONS):
        blocks_per_res = seq // dil // QB
        has_prev_block = blocks_per_res > 1
        n_keys = 2 * QB if has_prev_block else QB

        qi = lax.broadcasted_iota(jnp.int32, (QB, n_keys), 0)
        kj = lax.broadcasted_iota(jnp.int32, (QB, n_keys), 1)
        dist = qi + (n_keys - QB) - kj
        valid = jnp.logical_and(dist >= 0, dist <= N_STRIDES)
        distf = dist.astype(jnp.float32)
        for hh in range(2):
            slope = slopes_ref[g * N_HEADS + 2 * hp + hh] * (float(dil) * LOG2_E)
            table = jnp.where(valid, -slope * distf, NEG_INF)
            bias_ref[1, hh * QB:(hh + 1) * QB, 0:n_keys] = table
            if has_prev_block:
                bias_ref[0, hh * QB:(hh + 1) * QB, 0:n_keys] = jnp.where(kj >= QB, table, NEG_INF)

        def body(j, _, g=g, dil=dil, has_prev_block=has_prev_block, n_keys=n_keys):
            if dil == 1:
                start = pl.multiple_of(j * QB, QB)
                prev_start = jnp.maximum(start - QB, 0)
                rows = lambda s: pl.ds(s, QB)
                has_prev = jnp.minimum(j, 1)
            else:
                blk = j // dil
                res = j - blk * dil
                start = res + blk * (QB * dil)
                prev_start = jnp.maximum(start - QB * dil, 0)
                rows = lambda s: pl.ds(s, QB, stride=dil)
                has_prev = jnp.minimum(blk, 1)
            q2 = q_refs[g][rows(start), :] * q_scale
            qs = jnp.concatenate([jnp.where(first_head, q2, 0.0), jnp.where(first_head, 0.0, q2)], axis=0)
            if has_prev_block:
                kk = jnp.concatenate([k_ref[rows(prev_start), :], k_ref[rows(start), :]], axis=0)
                v_aug = jnp.concatenate(
                    [jnp.concatenate([va0_ref[rows(prev_start), :], va0_ref[rows(start), :]], axis=0),
                     jnp.concatenate([va1_ref[rows(prev_start), :], va1_ref[rows(start), :]], axis=0)], axis=1)
                bias = bias_ref[has_prev]
            else:
                kk = k_ref[rows(start), :]
                v_aug = jnp.concatenate([va0_ref[rows(start), :], va1_ref[rows(start), :]], axis=1)
                bias = bias_ref[1, :, 0:n_keys]
            s = lax.dot_general(qs.astype(bf16), kk.astype(bf16), nt, preferred_element_type=jnp.float32)
            s = s + bias
            mx = jnp.max(s, axis=-1, keepdims=True)
            p = jnp.exp2(s - mx).astype(bf16)
            r = jnp.dot(p, v_aug.astype(bf16), preferred_element_type=jnp.float32)
            r0 = r[:QB, :HEAD_PAIR]
            r1 = r[QB:, HEAD_PAIR:]
            accs[g][rows(start), :] = jnp.where(first_head, r0, r1)
            ls[g][rows(start), :] = pltpu.roll(jnp.where(first_head, r1, r0), HEAD_DIM, 1)
            ms[g][rows(start), :] = jnp.where(first_head, jnp.broadcast_to(mx[:QB], (QB, HEAD_PAIR)),
                                              jnp.broadcast_to(mx[QB:], (QB, HEAD_PAIR)))
            return 0

        lax.fori_loop(0, blocks_per_group, body, 0, unroll=ATTN_UNROLL)

    def merge(i, _):
        rows = pl.ds(pl.multiple_of(i * QB, QB), QB)
        mm = jnp.maximum(jnp.maximum(m0[rows, :], m1[rows, :]), m2[rows, :])
        num = jnp.zeros((QB, HEAD_PAIR), jnp.float32)
        den = jnp.zeros((QB, HEAD_PAIR), jnp.float32)
        for g in range(N_GROUPS):
            e = jnp.exp2(ms[g][rows, :] - mm)
            num = num + e * accs[g][rows, :]
            den = den + e * ls[g][rows, :]
        o_ref[rows, :] = ((num / den) * jax.nn.silu(z_ref[rows, :])).astype(jnp.bfloat16)
        return 0

    lax.fori_loop(0, seq // QB, merge, 0, unroll=2)


def prompt_attention(slopes, proj, kv):
    bsz, seq, _ = proj.shape
    n_pairs = ATTN_WIDTH // HEAD_PAIR
    col = lambda off: pl.BlockSpec((None, seq, HEAD_PAIR), lambda b, hp, off=off: (b, 0, off + hp))
    f32 = jnp.float32
    scratch = [pltpu.VMEM((seq, HEAD_PAIR), f32) for _ in range(11)]
    scratch.append(pltpu.VMEM((2, 2 * QB, 2 * QB), f32))
    return pl.pallas_call(
        functools.partial(_prompt_attn_kernel, seq=seq),
        grid=(bsz, n_pairs),
        in_specs=[pl.BlockSpec(memory_space=pltpu.SMEM),
                  col(0), col(n_pairs), col(2 * n_pairs), col(3 * n_pairs), col(0), col(n_pairs)],
        out_specs=pl.BlockSpec((None, seq, HEAD_PAIR), lambda b, hp: (b, 0, hp)),
        out_shape=jax.ShapeDtypeStruct((bsz, seq, ATTN_WIDTH), jnp.bfloat16),
        scratch_shapes=scratch,
        compiler_params=_params("parallel", "parallel"),
    )(slopes, proj, proj, proj, proj, kv, kv)


def _select_positions(x, lo, dil):
    width = x.shape[1] - lo
    pos = lax.broadcasted_iota(jnp.int32, (width, LANES), 0)
    col = lax.broadcasted_iota(jnp.int32, (width, LANES), 1)
    sel = jnp.where(pos == col * dil, 1.0, 0.0).astype(jnp.bfloat16)
    picked = jnp.dot(x[:, lo:].astype(jnp.bfloat16), sel, preferred_element_type=jnp.float32)
    return picked.astype(jnp.bfloat16)


def _cache_shift_kernel(c_ref, new_ref, o_ref, *win_refs):
    b = pl.program_id(0)
    rows, n_pos = c_ref.shape
    x = c_ref[...]
    o_ref[...] = pltpu.roll(x, n_pos - 1, 1)
    new = new_ref[...]
    lane = lax.broadcasted_iota(jnp.int32, new.shape, 1)
    new_col = jnp.sum(jnp.where(lane == b, new, 0.0), axis=1, keepdims=True)
    last = o_ref[:, n_pos - LANES:]
    last_lane = lax.broadcasted_iota(jnp.int32, last.shape, 1) == LANES - 1
    o_ref[:, n_pos - LANES:] = jnp.where(last_lane, new_col, last)
    for ref, dil in zip(win_refs, DILATIONS):
        if dil == 1:
            ref[...] = x[:, n_pos - LANES:].astype(jnp.bfloat16)
        else:
            ref[...] = _select_positions(x, n_pos - N_STRIDES * dil, dil)


def cache_shift(cache_t, new_t, *, rows=1024):
    bsz, chans, n_pos = cache_t.shape
    n_win = len(DILATIONS)
    win_spec = pl.BlockSpec((None, rows, LANES), lambda b, i: (b, i, 0))
    win_shape = jax.ShapeDtypeStruct((bsz, chans, LANES), jnp.bfloat16)
    return pl.pallas_call(
        _cache_shift_kernel,
        grid=(bsz, chans // rows),
        in_specs=[pl.BlockSpec((None, rows, n_pos), lambda b, i: (b, i, 0)),
                  pl.BlockSpec((rows, bsz), lambda b, i: (i, 0))],
        out_specs=[pl.BlockSpec((None, rows, n_pos), lambda b, i: (b, i, 0))] + [win_spec] * n_win,
        out_shape=[jax.ShapeDtypeStruct(cache_t.shape, cache_t.dtype)] + [win_shape] * n_win,
        compiler_params=_params("parallel", "parallel"),
    )(cache_t, new_t)


SEQS_PER_STEP = 8


def _sample_attn_kernel(bias_ref, proj_ref, new_ref, c1_ref, c4_ref, c16_ref, o_ref):
    packed = (c1_ref, c4_ref, c16_ref)
    head_of_chan = lax.broadcasted_iota(jnp.int32, (N_HEADS, ATTN_WIDTH), 1) // HEAD_DIM
    own = head_of_chan == lax.broadcasted_iota(jnp.int32, (N_HEADS, ATTN_WIDTH), 0)
    nt = (((1,), (1,)), ((), ()))
    bf16 = jnp.bfloat16

    def one_sequence(i, _):
        proj = proj_ref[i]
        new = new_ref[i]
        k_new = new[:, :ATTN_WIDTH]
        v_new = new[:, ATTN_WIDTH:]
        scores, new_scores = [], []
        for g in range(N_GROUPS):
            q = proj[:, g * ATTN_WIDTH:(g + 1) * ATTN_WIDTH] * (1.0 / math.sqrt(HEAD_DIM))
            q_bd = jnp.where(own, q, 0.0)
            k = packed[g][i, 0:ATTN_WIDTH, :]
            s = jnp.dot(q_bd.astype(bf16), k, preferred_element_type=jnp.float32) + bias_ref[g]
            scores.append(s)
            new_scores.append(jnp.sum(q_bd * k_new, axis=1, keepdims=True))
        mx = jnp.maximum(jnp.maximum(new_scores[0], new_scores[1]), new_scores[2])
        for g in range(N_GROUPS):
            mx = jnp.maximum(mx, jnp.max(scores[g], axis=1, keepdims=True))
        p_new = sum(jnp.exp(s - mx) for s in new_scores)
        den = p_new
        o = p_new * v_new
        for g in range(N_GROUPS):
            p = jnp.exp(scores[g] - mx)
            den = den + jnp.sum(p, axis=1, keepdims=True)
            v = packed[g][i, ATTN_WIDTH:2 * ATTN_WIDTH, :]
            o = o + lax.dot_general(p.astype(bf16), v, nt, preferred_element_type=jnp.float32)
        o_ref[i] = jnp.sum(jnp.where(own, o / den, 0.0), axis=0, keepdims=True)
        return 0

    lax.fori_loop(0, proj_ref.shape[0], one_sequence, 0)


def sample_attention(slopes, proj, kv_new, windows):
    bsz, chans, _ = windows[0].shape
    nb = SEQS_PER_STEP
    steps = (N_STRIDES - jnp.arange(LANES)).astype(jnp.float32)
    dil = jnp.asarray(DILATIONS, jnp.float32)
    bias = -(slopes.reshape(N_GROUPS, N_HEADS) * dil[:, None])[:, :, None] * steps
    row3 = lambda width: pl.BlockSpec((nb, 1, width), lambda b: (b, 0, 0))
    win = pl.BlockSpec((nb, chans, LANES), lambda b: (b, 0, 0))
    out = pl.pallas_call(
        _sample_attn_kernel,
        grid=(bsz // nb,),
        in_specs=[pl.BlockSpec((N_GROUPS, N_HEADS, LANES), lambda b: (0, 0, 0)),
                  row3(proj.shape[1]), row3(chans), win, win, win],
        out_specs=row3(ATTN_WIDTH),
        out_shape=jax.ShapeDtypeStruct((bsz, 1, ATTN_WIDTH), jnp.float32),
        compiler_params=_params("parallel"),
    )(bias, proj.reshape(bsz, 1, -1), kv_new.reshape(bsz, 1, chans), *windows)
    return out.reshape(bsz, ATTN_WIDTH)


def _alibi_slopes():
    n = N_GROUPS * N_HEADS
    e = jnp.arange(1, n + 1, dtype=jnp.float32)
    return jnp.exp2(-ALIBI_MAX_EXP * e / n)


def _ssm_layer(x, h0_re, h0_im, layer, norm_pre, w_in, ssm_w, w_glu, b_glu, w_out, norm_post, *, n_steps):
    uz = norm_matmul(x, norm_pre, w_in, layer)
    g, h_re, h_im = ssm_core(uz, h0_re.shape[0], h0_re=h0_re, h0_im=h0_im, n_steps=n_steps, **ssm_w)
    h2 = glu(g, w_glu, layer, b_glu.reshape(1, -1), uz)
    return out_proj(h2, w_out, layer, norm_post, x), h_re, h_im


def _states_out(h, bsz):
    return h.reshape(bsz, SSM_GROUPS, SSM_STATE)


def kernel(x_prompt, x_sample, state_s5_re, state_s5_im, cache_kv, a_norm_pre, a_w_in, a_lam_re, a_lam_im, a_log_dt, a_b_re, a_b_im, a_c_re, a_c_im, a_d, a_w_glu, a_b_glu, a_w_out, a_norm_post, kv_norm, w_kv, b_norm_pre, b_w_in, b_w_out, b_norm_post):
    bf16 = jnp.bfloat16
    n_a = a_w_in.shape[0]
    n_b = b_w_in.shape[0]
    bsz, seq, _ = x_prompt.shape
    dbsz = x_sample.shape[0]
    n_pos = cache_kv.shape[1]

    a_w_in_b = a_w_in.astype(bf16)
    a_w_glu_b = a_w_glu.astype(bf16)
    a_w_out_b = a_w_out.astype(bf16)
    w_kv_b = w_kv.astype(bf16)
    b_w_in_b = b_w_in.astype(bf16)
    b_w_out_b = b_w_out.astype(bf16)
    slopes = _alibi_slopes()
    ssm_ws = [ssm_weights(a_lam_re[l], a_lam_im[l], a_log_dt[l], a_b_re[l], a_b_im[l],
                           a_c_re[l], a_c_im[l], a_d[l]) for l in range(n_a)]

    x = jnp.swapaxes(x_prompt, 0, 1).reshape(seq * bsz, D_MODEL)
    zeros = jnp.zeros((bsz, SSM_GROUPS * SSM_STATE), jnp.float32)
    p_re, p_im = [], []
    for l in range(n_a):
        x, h_re, h_im = _ssm_layer(x, zeros, zeros, l, a_norm_pre[l], a_w_in_b, ssm_ws[l], a_w_glu_b,
                                   a_b_glu[l], a_w_out_b, a_norm_post[l], n_steps=128)
        p_re.append(_states_out(h_re, bsz))
        p_im.append(_states_out(h_im, bsz))
    x = jnp.swapaxes(x.reshape(seq, bsz, D_MODEL), 0, 1)
    kv, kv_t = kv_project(x, kv_norm, w_kv_b)
    x = x.reshape(bsz * seq, D_MODEL)
    for l in range(n_b):
        proj = norm_matmul(x, b_norm_pre[l], b_w_in_b, l)
        o = prompt_attention(slopes, proj.reshape(bsz, seq, -1), kv)
        x = out_proj(o.reshape(bsz * seq, ATTN_WIDTH), b_w_out_b, l, b_norm_post[l], x)
    y_prompt = x.reshape(bsz, seq, D_MODEL)
    prompt_kv = jnp.transpose(kv_t.reshape(bsz, 2, N_HEADS, HEAD_DIM, seq), (0, 4, 1, 2, 3))

    xs = x_sample.reshape(dbsz, D_MODEL)
    s_re, s_im = [], []
    for l in range(n_a):
        h0_re = state_s5_re[l].reshape(dbsz, SSM_GROUPS * SSM_STATE)
        h0_im = state_s5_im[l].reshape(dbsz, SSM_GROUPS * SSM_STATE)
        xs, h_re, h_im = _ssm_layer(xs, h0_re, h0_im, l, a_norm_pre[l], a_w_in_b, ssm_ws[l], a_w_glu_b,
                                    a_b_glu[l], a_w_out_b, a_norm_post[l], n_steps=1)
        s_re.append(_states_out(h_re, dbsz))
        s_im.append(_states_out(h_im, dbsz))
    kv_s, kv_s_t = kv_project(xs[None], kv_norm, w_kv_b)
    cache_t = jnp.transpose(cache_kv, (0, 2, 3, 4, 1)).reshape(dbsz, 2 * ATTN_WIDTH, n_pos)
    new_cache_t, *windows = cache_shift(cache_t, kv_s_t[0])
    sample_kv = jnp.transpose(new_cache_t.reshape(dbsz, 2, N_HEADS, HEAD_DIM, n_pos), (0, 4, 1, 2, 3))

    for l in range(n_b):
        proj = norm_matmul(xs, b_norm_pre[l], b_w_in_b, l)
        o = sample_attention(slopes, proj, kv_s[0], windows)
        xs = gated_out_proj(o, proj, N_GROUPS, b_w_out_b, l, b_norm_post[l], xs)
    y_sample = xs.reshape(dbsz, 1, D_MODEL)

    return (y_prompt, y_sample, jnp.stack(p_re), jnp.stack(p_im), prompt_kv,
            jnp.stack(s_re), jnp.stack(s_im), sample_kv)
```

```python
import functools
import math

import jax
import jax.numpy as jnp
from jax import lax
from jax.experimental import pallas as pl
from jax.experimental.pallas import tpu as pltpu

D_MODEL = 1024
SSM_WIDTH = 2048
SSM_GROUP = 16
SSM_GROUPS = SSM_WIDTH // SSM_GROUP
SSM_STATE = 64
HEAD_DIM = 64
N_HEADS = 16
ATTN_WIDTH = N_HEADS * HEAD_DIM
DILATIONS = (1, 4, 16)
N_STRIDES = 128
N_GROUPS = len(DILATIONS)
ALIBI_MAX_EXP = 8.0
RMS_EPS = 1e-6
NEG_INF = -1e30

VMEM_LIMIT_BYTES_V7X = 56 * 1024 * 1024
LANES = 128

GROUPS_PER_BLOCK = 16
N_GROUP_BLOCKS = SSM_GROUPS // GROUPS_PER_BLOCK
BLOCK_CHANNELS = GROUPS_PER_BLOCK * SSM_GROUP
BLOCK_STATES = GROUPS_PER_BLOCK * SSM_STATE
SSM_STAGES = 2

ROW_TILE = 512


def _params(*semantics):
    return pltpu.CompilerParams(dimension_semantics=semantics,
                                vmem_limit_bytes=VMEM_LIMIT_BYTES_V7X)


def _rms_scale(x):
    return lax.rsqrt(jnp.mean(x * x, axis=-1, keepdims=True) + RMS_EPS)


def _norm_matmul_kernel(x_ref, g_ref, w_ref, o_ref):
    x = x_ref[...]
    xn = (x * _rms_scale(x) * g_ref[...]).astype(jnp.bfloat16)
    o_ref[...] = jnp.dot(xn, w_ref[...], preferred_element_type=jnp.float32)


def norm_matmul(x, g, w, layer):
    rows, d = x.shape
    n = w.shape[2]
    tm = min(ROW_TILE, rows)
    return pl.pallas_call(
        _norm_matmul_kernel,
        grid=(rows // tm,),
        in_specs=[
            pl.BlockSpec((tm, d), lambda i: (i, 0)),
            pl.BlockSpec((1, d), lambda i: (0, 0)),
            pl.BlockSpec((None, d, n), lambda i: (layer, 0, 0)),
        ],
        out_specs=pl.BlockSpec((tm, n), lambda i: (i, 0)),
        out_shape=jax.ShapeDtypeStruct((rows, n), jnp.float32),
        compiler_params=_params("parallel"),
    )(x, g.reshape(1, d), w)


def _kv_kernel(x_ref, g_ref, w_ref, kv_ref, kvt_ref):
    x = x_ref[...]
    xn = (x * _rms_scale(x) * g_ref[...]).astype(jnp.bfloat16)
    kv = jnp.dot(xn, w_ref[...], preferred_element_type=jnp.float32)
    kv_ref[...] = kv
    kvt_ref[...] = kv.T


def kv_project(x, g, w, *, tm=256):
    bsz, seq, d = x.shape
    n = w.shape[1]
    tm = min(tm, seq)
    return pl.pallas_call(
        _kv_kernel,
        grid=(bsz, seq // tm),
        in_specs=[
            pl.BlockSpec((None, tm, d), lambda b, i: (b, i, 0)),
            pl.BlockSpec((1, d), lambda b, i: (0, 0)),
            pl.BlockSpec((d, n), lambda b, i: (0, 0)),
        ],
        out_specs=[
            pl.BlockSpec((None, tm, n), lambda b, i: (b, i, 0)),
            pl.BlockSpec((None, n, tm), lambda b, i: (b, 0, i)),
        ],
        out_shape=[
            jax.ShapeDtypeStruct((bsz, seq, n), jnp.float32),
            jax.ShapeDtypeStruct((bsz, n, seq), jnp.float32),
        ],
        compiler_params=_params("parallel", "parallel"),
    )(x, g.reshape(1, d), w)


def _ssm_weights_kernel(lam_re_ref, lam_im_ref, log_dt_ref, bt_re_ref, bt_im_ref, ct_re_ref, ct_im_ref,
                        ab_re_ref, ab_im_ref, bre_ref, bim_ref, cre_ref, cim_ref):
    lam_re = lam_re_ref[...]
    lam_im = lam_im_ref[...]
    dt = jnp.exp(log_dt_ref[...])
    mag = jnp.exp(lam_re * dt)
    ab_re = mag * jnp.cos(lam_im * dt)
    ab_im = mag * jnp.sin(lam_im * dt)
    den = lam_re * lam_re + lam_im * lam_im
    num_re = ab_re - 1.0
    coef_re = (num_re * lam_re + ab_im * lam_im) / den
    coef_im = (ab_im * lam_re - num_re * lam_im) / den
    ab_re_ref[...] = ab_re
    ab_im_ref[...] = ab_im
    b_re = bt_re_ref[...]
    b_im = bt_im_ref[...]
    cr = coef_re[:, None, :]
    ci = coef_im[:, None, :]
    bbt_re = cr * b_re - ci * b_im
    bbt_im = cr * b_im + ci * b_re

    bf16 = jnp.bfloat16

    def block_diag(w, rows, cols, rows_per_group, cols_per_group):
        src = lax.broadcasted_iota(jnp.int32, (cols_per_group, cols), 0)
        dst = lax.broadcasted_iota(jnp.int32, (cols_per_group, cols), 1)
        spread = jnp.where((dst & (cols_per_group - 1)) == src, 1.0, 0.0).astype(bf16)
        tiled = jnp.dot(w.astype(bf16), spread, preferred_element_type=jnp.float32)
        r = lax.broadcasted_iota(jnp.int32, (rows, cols), 0) // rows_per_group
        c = lax.broadcasted_iota(jnp.int32, (rows, cols), 1) // cols_per_group
        return jnp.where(r == c, tiled, 0.0).astype(bf16)

    gc, p = b_re.shape[1], b_re.shape[2]
    bre_ref[...] = block_diag(bbt_re.reshape(BLOCK_CHANNELS, p), BLOCK_CHANNELS, BLOCK_STATES, gc, p)
    bim_ref[...] = block_diag(bbt_im.reshape(BLOCK_CHANNELS, p), BLOCK_CHANNELS, BLOCK_STATES, gc, p)
    cre_ref[...] = block_diag(ct_re_ref[...].reshape(BLOCK_STATES, gc), BLOCK_STATES, BLOCK_CHANNELS, p, gc)
    cim_ref[...] = block_diag(ct_im_ref[...].reshape(BLOCK_STATES, gc), BLOCK_STATES, BLOCK_CHANNELS, p, gc)


def ssm_weights(lam_re, lam_im, log_dt, b_re, b_im, c_re, c_im, d_skip):
    g, p = lam_re.shape
    gc = b_re.shape[-1]
    gpb = GROUPS_PER_BLOCK
    f32, bf16 = jnp.float32, jnp.bfloat16
    grp2 = lambda w: pl.BlockSpec((gpb, w), lambda i: (i, 0))
    grp3 = lambda a, b: pl.BlockSpec((gpb, a, b), lambda i: (i, 0, 0))
    blk = lambda a, b: pl.BlockSpec((None, a, b), lambda i: (i, 0, 0))
    ab_re, ab_im, bre, bim, cre, cim = pl.pallas_call(
        _ssm_weights_kernel,
        grid=(N_GROUP_BLOCKS,),
        in_specs=[grp2(p), grp2(p), grp2(1), grp3(gc, p), grp3(gc, p), grp3(p, gc), grp3(p, gc)],
        out_specs=[grp2(p), grp2(p), blk(BLOCK_CHANNELS, BLOCK_STATES), blk(BLOCK_CHANNELS, BLOCK_STATES),
                   blk(BLOCK_STATES, BLOCK_CHANNELS), blk(BLOCK_STATES, BLOCK_CHANNELS)],
        out_shape=[jax.ShapeDtypeStruct((g, p), f32), jax.ShapeDtypeStruct((g, p), f32),
                   jax.ShapeDtypeStruct((N_GROUP_BLOCKS, BLOCK_CHANNELS, BLOCK_STATES), bf16),
                   jax.ShapeDtypeStruct((N_GROUP_BLOCKS, BLOCK_CHANNELS, BLOCK_STATES), bf16),
                   jax.ShapeDtypeStruct((N_GROUP_BLOCKS, BLOCK_STATES, BLOCK_CHANNELS), bf16),
                   jax.ShapeDtypeStruct((N_GROUP_BLOCKS, BLOCK_STATES, BLOCK_CHANNELS), bf16)],
        compiler_params=_params("parallel"),
    )(lam_re, lam_im, log_dt.reshape(g, 1), jnp.swapaxes(b_re, 1, 2), jnp.swapaxes(b_im, 1, 2),
      jnp.swapaxes(c_re, 1, 2), jnp.swapaxes(c_im, 1, 2))
    return dict(bre=bre, bim=bim, cre=cre, cim=cim,
                ar=ab_re.reshape(N_GROUP_BLOCKS, 1, BLOCK_STATES), ai=ab_im.reshape(N_GROUP_BLOCKS, 1, BLOCK_STATES),
                d_skip=d_skip.reshape(N_GROUP_BLOCKS, 1, BLOCK_CHANNELS))


def _ssm_kernel(u_ref, bre_ref, bim_ref, cre_ref, cim_ref, ar_ref, ai_ref, d_ref,
                h0re_ref, h0im_ref, g_ref, hre_out_ref, him_out_ref,
                sre_ref, sim_ref, hre_ref, him_ref, *, n_steps, bsz):
    tc = pl.program_id(1)

    @pl.when(tc == 0)
    def _():
        hre_ref[...] = h0re_ref[...]
        him_ref[...] = h0im_ref[...]

    ar = jnp.broadcast_to(ar_ref[...], (bsz, BLOCK_STATES))
    ai = jnp.broadcast_to(ai_ref[...], (bsz, BLOCK_STATES))

    n_stages = min(SSM_STAGES, n_steps)
    steps_per_stage = n_steps // n_stages
    stage_rows = lambda s: slice(s * steps_per_stage * bsz, (s + 1) * steps_per_stage * bsz)

    def project_in(s):
        ub = u_ref[stage_rows(s), :].astype(jnp.bfloat16)
        sre_ref[stage_rows(s), :] = jnp.dot(ub, bre_ref[...], preferred_element_type=jnp.float32)
        sim_ref[stage_rows(s), :] = jnp.dot(ub, bim_ref[...], preferred_element_type=jnp.float32)

    def recur(s, hr, hi):
        for t in range(s * steps_per_stage, (s + 1) * steps_per_stage):
            rows = slice(t * bsz, (t + 1) * bsz)
            nhr = ar * hr - ai * hi + sre_ref[rows, :]
            nhi = ar * hi + ai * hr + sim_ref[rows, :]
            sre_ref[rows, :] = nhr
            sim_ref[rows, :] = nhi
            hr, hi = nhr, nhi
        return hr, hi

    def project_out(s):
        rows = stage_rows(s)
        y = jnp.dot(sre_ref[rows, :].astype(jnp.bfloat16), cre_ref[...], preferred_element_type=jnp.float32)
        y = y - jnp.dot(sim_ref[rows, :].astype(jnp.bfloat16), cim_ref[...], preferred_element_type=jnp.float32)
        y = y + d_ref[...] * u_ref[rows, :]
        g_ref[rows, :] = jax.nn.gelu(y).astype(jnp.bfloat16)

    hr, hi = hre_ref[...], him_ref[...]
    project_in(0)
    for s in range(n_stages):
        if s + 1 < n_stages:
            project_in(s + 1)
        hr, hi = recur(s, hr, hi)
        project_out(s)
    hre_ref[...] = hr
    him_ref[...] = hi

    @pl.when(tc == pl.num_programs(1) - 1)
    def _():
        hre_out_ref[...] = hr
        him_out_ref[...] = hi


def ssm_core(uz, bsz, bre, bim, cre, cim, ar, ai, d_skip, h0_re, h0_im, *, n_steps):
    rows = uz.shape[0]
    seq = rows // bsz
    n_steps = min(n_steps, seq)
    tr = n_steps * bsz
    kern = functools.partial(_ssm_kernel, n_steps=n_steps, bsz=bsz)
    f32 = jnp.float32
    wspec = lambda shape: pl.BlockSpec((None,) + shape, lambda gb, tc: (gb, 0, 0))
    return pl.pallas_call(
        kern,
        grid=(N_GROUP_BLOCKS, seq // n_steps),
        in_specs=[
            pl.BlockSpec((tr, BLOCK_CHANNELS), lambda gb, tc: (tc, gb)),
            wspec((BLOCK_CHANNELS, BLOCK_STATES)), wspec((BLOCK_CHANNELS, BLOCK_STATES)),
            wspec((BLOCK_STATES, BLOCK_CHANNELS)), wspec((BLOCK_STATES, BLOCK_CHANNELS)),
            wspec((1, BLOCK_STATES)), wspec((1, BLOCK_STATES)), wspec((1, BLOCK_CHANNELS)),
            pl.BlockSpec((bsz, BLOCK_STATES), lambda gb, tc: (0, gb)),
            pl.BlockSpec((bsz, BLOCK_STATES), lambda gb, tc: (0, gb)),
        ],
        out_specs=[
            pl.BlockSpec((tr, BLOCK_CHANNELS), lambda gb, tc: (tc, gb)),
            pl.BlockSpec((bsz, BLOCK_STATES), lambda gb, tc: (0, gb)),
            pl.BlockSpec((bsz, BLOCK_STATES), lambda gb, tc: (0, gb)),
        ],
        out_shape=[
            jax.ShapeDtypeStruct((rows, SSM_WIDTH), jnp.bfloat16),
            jax.ShapeDtypeStruct((bsz, SSM_GROUPS * SSM_STATE), f32),
            jax.ShapeDtypeStruct((bsz, SSM_GROUPS * SSM_STATE), f32),
        ],
        scratch_shapes=[pltpu.VMEM((tr, BLOCK_STATES), f32), pltpu.VMEM((tr, BLOCK_STATES), f32),
                        pltpu.VMEM((bsz, BLOCK_STATES), f32), pltpu.VMEM((bsz, BLOCK_STATES), f32)],
        compiler_params=_params("parallel", "arbitrary"),
    )(uz, bre, bim, cre, cim, ar, ai, d_skip, h0_re, h0_im)


def _glu_kernel(g_ref, wa_ref, wb_ref, ba_ref, bb_ref, z_ref, o_ref):
    g = g_ref[...]
    a = jnp.dot(g, wa_ref[...], preferred_element_type=jnp.float32) + ba_ref[...]
    b = jnp.dot(g, wb_ref[...], preferred_element_type=jnp.float32) + bb_ref[...]
    o_ref[...] = (a * jax.nn.sigmoid(b) * jax.nn.silu(z_ref[...])).astype(jnp.bfloat16)


def glu(g, w_glu, layer, b_glu, uz, *, tn=512):
    rows, e = g.shape
    tm = min(ROW_TILE, rows)
    nj = e // tn
    return pl.pallas_call(
        _glu_kernel,
        grid=(nj, rows // tm),
        in_specs=[
            pl.BlockSpec((tm, e), lambda j, i: (i, 0)),
            pl.BlockSpec((None, e, tn), lambda j, i: (layer, 0, j)),
            pl.BlockSpec((None, e, tn), lambda j, i: (layer, 0, nj + j)),
            pl.BlockSpec((1, tn), lambda j, i: (0, j)),
            pl.BlockSpec((1, tn), lambda j, i: (0, nj + j)),
            pl.BlockSpec((tm, tn), lambda j, i: (i, nj + j)),
        ],
        out_specs=pl.BlockSpec((tm, tn), lambda j, i: (i, j)),
        out_shape=jax.ShapeDtypeStruct((rows, e), jnp.bfloat16),
        compiler_params=_params("parallel", "parallel"),
    )(g, w_glu, w_glu, b_glu, b_glu, uz)


def _out_proj_kernel(h_ref, w_ref, g_ref, x_ref, o_ref):
    y = jnp.dot(h_ref[...], w_ref[...], preferred_element_type=jnp.float32)
    o_ref[...] = x_ref[...] + y * _rms_scale(y) * g_ref[...]


def _gated_out_proj_kernel(a_ref, z_ref, w_ref, g_ref, x_ref, o_ref):
    h = (a_ref[...] * jax.nn.silu(z_ref[...])).astype(jnp.bfloat16)
    y = jnp.dot(h, w_ref[...], preferred_element_type=jnp.float32)
    o_ref[...] = x_ref[...] + y * _rms_scale(y) * g_ref[...]


def out_proj(h, w, layer, g, x):
    rows, d = x.shape
    k = w.shape[1]
    tm = min(ROW_TILE, rows)
    return pl.pallas_call(
        _out_proj_kernel,
        grid=(rows // tm,),
        in_specs=[
            pl.BlockSpec((tm, k), lambda i: (i, 0)),
            pl.BlockSpec((None, k, d), lambda i: (layer, 0, 0)),
            pl.BlockSpec((1, d), lambda i: (0, 0)),
            pl.BlockSpec((tm, d), lambda i: (i, 0)),
        ],
        out_specs=pl.BlockSpec((tm, d), lambda i: (i, 0)),
        out_shape=jax.ShapeDtypeStruct((rows, d), jnp.float32),
        compiler_params=_params("parallel"),
    )(h, w, g.reshape(1, d), x)


def gated_out_proj(a, proj, z_block, w, layer, g, x):
    rows, d = x.shape
    k = w.shape[1]
    return pl.pallas_call(
        _gated_out_proj_kernel,
        grid=(1,),
        in_specs=[
            pl.BlockSpec((rows, k), lambda i: (0, 0)),
            pl.BlockSpec((rows, k), lambda i: (0, z_block)),
            pl.BlockSpec((None, k, d), lambda i: (layer, 0, 0)),
            pl.BlockSpec((1, d), lambda i: (0, 0)),
            pl.BlockSpec((rows, d), lambda i: (0, 0)),
        ],
        out_specs=pl.BlockSpec((rows, d), lambda i: (0, 0)),
        out_shape=jax.ShapeDtypeStruct((rows, d), jnp.float32),
        compiler_params=_params("arbitrary"),
    )(a, proj, w, g.reshape(1, d), x)


QB = N_STRIDES
HEAD_PAIR = 2 * HEAD_DIM
LOG2_E = math.log2(math.e)
ATTN_UNROLL = 16


def _prompt_attn_kernel(slopes_ref, q0_ref, q1_ref, q2_ref, z_ref, k_ref, v_ref, o_ref,
                        acc0, acc1, acc2, m0, m1, m2, l0, l1, l2, va0_ref, va1_ref, bias_ref, *, seq):
    hp = pl.program_id(1)
    q_refs = (q0_ref, q1_ref, q2_ref)
    accs, ms, ls = (acc0, acc1, acc2), (m0, m1, m2), (l0, l1, l2)
    first_head = lax.broadcasted_iota(jnp.int32, (QB, HEAD_PAIR), 1) < HEAD_DIM
    nt = (((1,), (1,)), ((), ()))
    blocks_per_group = seq // QB
    bf16 = jnp.bfloat16
    q_scale = LOG2_E / math.sqrt(HEAD_DIM)

    def fill_values(i, _):
        rows = pl.ds(pl.multiple_of(i * QB, QB), QB)
        v = v_ref[rows, :]
        va0_ref[rows, :] = jnp.where(first_head, v, 1.0)
        va1_ref[rows, :] = jnp.where(first_head, 1.0, v)
        return 0

    lax.fori_loop(0, seq // QB, fill_values, 0, unroll=2)

    for g, dil in enumerate(DILATIONS):
        blocks_per_res = seq // dil // QB
        has_prev_block = blocks_per_res > 1
        n_keys = 2 * QB if has_prev_block else QB

        qi = lax.broadcasted_iota(jnp.int32, (QB, n_keys), 0)
        kj = lax.broadcasted_iota(jnp.int32, (QB, n_keys), 1)
        dist = qi + (n_keys - QB) - kj
        valid = jnp.logical_and(dist >= 0, dist <= N_STRIDES)
        distf = dist.astype(jnp.float32)
        for hh in range(2):
            slope = slopes_ref[g * N_HEADS + 2 * hp + hh] * (float(dil) * LOG2_E)
            table = jnp.where(valid, -slope * distf, NEG_INF)
            bias_ref[1, hh * QB:(hh + 1) * QB, 0:n_keys] = table
            if has_prev_block:
                bias_ref[0, hh * QB:(hh + 1) * QB, 0:n_keys] = jnp.where(kj >= QB, table, NEG_INF)

        def body(j, _, g=g, dil=dil, has_prev_block=has_prev_block, n_keys=n_keys):
            if dil == 1:
                start = pl.multiple_of(j * QB, QB)
                prev_start = jnp.maximum(start - QB, 0)
                rows = lambda s: pl.ds(s, QB)
                has_prev = jnp.minimum(j, 1)
            else:
                blk = j // dil
                res = j - blk * dil
                start = res + blk * (QB * dil)
                prev_start = jnp.maximum(start - QB * dil, 0)
                rows = lambda s: pl.ds(s, QB, stride=dil)
                has_prev = jnp.minimum(blk, 1)
            q2 = q_refs[g][rows(start), :] * q_scale
            qs = jnp.concatenate([jnp.where(first_head, q2, 0.0), jnp.where(first_head, 0.0, q2)], axis=0)
            if has_prev_block:
                kk = jnp.concatenate([k_ref[rows(prev_start), :], k_ref[rows(start), :]], axis=0)
                v_aug = jnp.concatenate(
                    [jnp.concatenate([va0_ref[rows(prev_start), :], va0_ref[rows(start), :]], axis=0),
                     jnp.concatenate([va1_ref[rows(prev_start), :], va1_ref[rows(start), :]], axis=0)], axis=1)
                bias = bias_ref[has_prev]
            else:
                kk = k_ref[rows(start), :]
                v_aug = jnp.concatenate([va0_ref[rows(start), :], va1_ref[rows(start), :]], axis=1)
                bias = bias_ref[1, :, 0:n_keys]
            s = lax.dot_general(qs.astype(bf16), kk.astype(bf16), nt, preferred_element_type=jnp.float32)
            s = s + bias
            mx = jnp.max(s, axis=-1, keepdims=True)
            p = jnp.exp2(s - mx).astype(bf16)
            r = jnp.dot(p, v_aug.astype(bf16), preferred_element_type=jnp.float32)
            r0 = r[:QB, :HEAD_PAIR]
            r1 = r[QB:, HEAD_PAIR:]
            accs[g][rows(start), :] = jnp.where(first_head, r0, r1)
            ls[g][rows(start), :] = pltpu.roll(jnp.where(first_head, r1, r0), HEAD_DIM, 1)
            ms[g][rows(start), :] = jnp.where(first_head, jnp.broadcast_to(mx[:QB], (QB, HEAD_PAIR)),
                                              jnp.broadcast_to(mx[QB:], (QB, HEAD_PAIR)))
            return 0

        lax.fori_loop(0, blocks_per_group, body, 0, unroll=ATTN_UNROLL)

    def merge(i, _):
        rows = pl.ds(pl.multiple_of(i * QB, QB), QB)
        mm = jnp.maximum(jnp.maximum(m0[rows, :], m1[rows, :]), m2[rows, :])
        num = jnp.zeros((QB, HEAD_PAIR), jnp.float32)
        den = jnp.zeros((QB, HEAD_PAIR), jnp.float32)
        for g in range(N_GROUPS):
            e = jnp.exp2(ms[g][rows, :] - mm)
            num = num + e * accs[g][rows, :]
            den = den + e * ls[g][rows, :]
        o_ref[rows, :] = ((num / den) * jax.nn.silu(z_ref[rows, :])).astype(jnp.bfloat16)
        return 0

    lax.fori_loop(0, seq // QB, merge, 0, unroll=2)


def prompt_attention(slopes, proj, kv):
    bsz, seq, _ = proj.shape
    n_pairs = ATTN_WIDTH // HEAD_PAIR
    col = lambda off: pl.BlockSpec((None, seq, HEAD_PAIR), lambda b, hp, off=off: (b, 0, off + hp))
    f32 = jnp.float32
    scratch = [pltpu.VMEM((seq, HEAD_PAIR), f32) for _ in range(11)]
    scratch.append(pltpu.VMEM((2, 2 * QB, 2 * QB), f32))
    return pl.pallas_call(
        functools.partial(_prompt_attn_kernel, seq=seq),
        grid=(bsz, n_pairs),
        in_specs=[pl.BlockSpec(memory_space=pltpu.SMEM),
                  col(0), col(n_pairs), col(2 * n_pairs), col(3 * n_pairs), col(0), col(n_pairs)],
        out_specs=pl.BlockSpec((None, seq, HEAD_PAIR), lambda b, hp: (b, 0, hp)),
        out_shape=jax.ShapeDtypeStruct((bsz, seq, ATTN_WIDTH), jnp.bfloat16),
        scratch_shapes=scratch,
        compiler_params=_params("parallel", "parallel"),
    )(slopes, proj, proj, proj, proj, kv, kv)


def _select_positions(x, lo, dil):
    width = x.shape[1] - lo
    pos = lax.broadcasted_iota(jnp.int32, (width, LANES), 0)
    col = lax.broadcasted_iota(jnp.int32, (width, LANES), 1)
    sel = jnp.where(pos == col * dil, 1.0, 0.0).astype(jnp.bfloat16)
    picked = jnp.dot(x[:, lo:].astype(jnp.bfloat16), sel, preferred_element_type=jnp.float32)
    return picked.astype(jnp.bfloat16)


def _cache_shift_kernel(b, c_ref, new_ref, o_ref, *win_refs):
    rows, n_pos = c_ref.shape
    x = c_ref[...]
    o_ref[...] = pltpu.roll(x, n_pos - 1, 1)
    new = new_ref[...]
    lane = lax.broadcasted_iota(jnp.int32, new.shape, 1)
    new_col = jnp.sum(jnp.where(lane == b, new, 0.0), axis=1, keepdims=True)
    last = o_ref[:, n_pos - LANES:]
    last_lane = lax.broadcasted_iota(jnp.int32, last.shape, 1) == LANES - 1
    o_ref[:, n_pos - LANES:] = jnp.where(last_lane, new_col, last)
    for ref, dil in zip(win_refs, DILATIONS):
        if dil == 1:
            ref[...] = x[:, n_pos - LANES:].astype(jnp.bfloat16)
        else:
            ref[...] = _select_positions(x, n_pos - N_STRIDES * dil, dil)


SHIFT_ROWS = 512
GLU_SHIFT_ROW_TILE = 256


def _glu_shift_kernel(g_ref, wa_ref, wb_ref, ba_ref, bb_ref, z_ref, c_ref, new_ref, *refs,
                      n_aliased, batch_offset, row_tiles, blocks_per_seq):
    o_ref, oc_ref, *win_refs = refs[n_aliased:]
    _glu_kernel(g_ref, wa_ref, wb_ref, ba_ref, bb_ref, z_ref, o_ref)
    step = pl.program_id(0) * row_tiles + pl.program_id(1)
    _cache_shift_kernel(batch_offset + step // blocks_per_seq, c_ref, new_ref, oc_ref, *win_refs)


def glu_and_cache_shift(g, w_glu, layer, b_glu, uz, cache_t, new_t, batch_offset, shifted=None, *, tn=512):
    rows, e = g.shape
    bsz, chans, n_pos = cache_t.shape
    tm = GLU_SHIFT_ROW_TILE
    nj, ni = e // tn, rows // tm
    bps = chans // SHIFT_ROWS
    assert (nj * ni) % bps == 0 and batch_offset + (nj * ni) // bps <= bsz
    n_win = len(DILATIONS)
    seq_of = lambda j, i: batch_offset + (j * ni + i) // bps
    blk_of = lambda j, i: (j * ni + i) % bps
    cache_spec = pl.BlockSpec((None, SHIFT_ROWS, n_pos), lambda j, i: (seq_of(j, i), blk_of(j, i), 0))
    win_spec = pl.BlockSpec((None, SHIFT_ROWS, LANES), lambda j, i: (seq_of(j, i), blk_of(j, i), 0))
    aliased = [] if shifted is None else list(shifted)
    n_glu_in = 8
    kern = functools.partial(_glu_shift_kernel, n_aliased=len(aliased), batch_offset=batch_offset,
                             row_tiles=ni, blocks_per_seq=bps)
    return pl.pallas_call(
        kern,
        grid=(nj, ni),
        in_specs=[
            pl.BlockSpec((tm, e), lambda j, i: (i, 0)),
            pl.BlockSpec((None, e, tn), lambda j, i: (layer, 0, j)),
            pl.BlockSpec((None, e, tn), lambda j, i: (layer, 0, nj + j)),
            pl.BlockSpec((1, tn), lambda j, i: (0, j)),
            pl.BlockSpec((1, tn), lambda j, i: (0, nj + j)),
            pl.BlockSpec((tm, tn), lambda j, i: (i, nj + j)),
            cache_spec,
            pl.BlockSpec((SHIFT_ROWS, bsz), lambda j, i: (blk_of(j, i), 0)),
        ] + [pl.BlockSpec(memory_space=pl.ANY)] * len(aliased),
        out_specs=[pl.BlockSpec((tm, tn), lambda j, i: (i, j)), cache_spec] + [win_spec] * n_win,
        out_shape=[jax.ShapeDtypeStruct((rows, e), jnp.bfloat16),
                   jax.ShapeDtypeStruct(cache_t.shape, cache_t.dtype)]
                  + [jax.ShapeDtypeStruct((bsz, chans, LANES), jnp.bfloat16)] * n_win,
        input_output_aliases={n_glu_in + k: 1 + k for k in range(len(aliased))},
        compiler_params=_params("parallel", "parallel"),
    )(g, w_glu, w_glu, b_glu, b_glu, uz, cache_t, new_t, *aliased)


SEQS_PER_STEP = 8


def _sample_attn_kernel(bias_ref, proj_ref, new_ref, c1_ref, c4_ref, c16_ref, o_ref):
    packed = (c1_ref, c4_ref, c16_ref)
    head_of_chan = lax.broadcasted_iota(jnp.int32, (N_HEADS, ATTN_WIDTH), 1) // HEAD_DIM
    own = head_of_chan == lax.broadcasted_iota(jnp.int32, (N_HEADS, ATTN_WIDTH), 0)
    nt = (((1,), (1,)), ((), ()))
    bf16 = jnp.bfloat16

    def one_sequence(i, _):
        proj = proj_ref[i]
        new = new_ref[i]
        k_new = new[:, :ATTN_WIDTH]
        v_new = new[:, ATTN_WIDTH:]
        scores, new_scores = [], []
        for g in range(N_GROUPS):
            q = proj[:, g * ATTN_WIDTH:(g + 1) * ATTN_WIDTH] * (1.0 / math.sqrt(HEAD_DIM))
            q_bd = jnp.where(own, q, 0.0)
            k = packed[g][i, 0:ATTN_WIDTH, :]
            s = jnp.dot(q_bd.astype(bf16), k, preferred_element_type=jnp.float32) + bias_ref[g]
            scores.append(s)
            new_scores.append(jnp.sum(q_bd * k_new, axis=1, keepdims=True))
        mx = jnp.maximum(jnp.maximum(new_scores[0], new_scores[1]), new_scores[2])
        for g in range(N_GROUPS):
            mx = jnp.maximum(mx, jnp.max(scores[g], axis=1, keepdims=True))
        p_new = sum(jnp.exp(s - mx) for s in new_scores)
        den = p_new
        o = p_new * v_new
        for g in range(N_GROUPS):
            p = jnp.exp(scores[g] - mx)
            den = den + jnp.sum(p, axis=1, keepdims=True)
            v = packed[g][i, ATTN_WIDTH:2 * ATTN_WIDTH, :]
            o = o + lax.dot_general(p.astype(bf16), v, nt, preferred_element_type=jnp.float32)
        o_ref[i] = jnp.sum(jnp.where(own, o / den, 0.0), axis=0, keepdims=True)
        return 0

    lax.fori_loop(0, proj_ref.shape[0], one_sequence, 0)


def sample_attention(slopes, proj, kv_new, windows):
    bsz, chans, _ = windows[0].shape
    nb = SEQS_PER_STEP
    steps = (N_STRIDES - jnp.arange(LANES)).astype(jnp.float32)
    dil = jnp.asarray(DILATIONS, jnp.float32)
    bias = -(slopes.reshape(N_GROUPS, N_HEADS) * dil[:, None])[:, :, None] * steps
    row3 = lambda width: pl.BlockSpec((nb, 1, width), lambda b: (b, 0, 0))
    win = pl.BlockSpec((nb, chans, LANES), lambda b: (b, 0, 0))
    out = pl.pallas_call(
        _sample_attn_kernel,
        grid=(bsz // nb,),
        in_specs=[pl.BlockSpec((N_GROUPS, N_HEADS, LANES), lambda b: (0, 0, 0)),
                  row3(proj.shape[1]), row3(chans), win, win, win],
        out_specs=row3(ATTN_WIDTH),
        out_shape=jax.ShapeDtypeStruct((bsz, 1, ATTN_WIDTH), jnp.float32),
        compiler_params=_params("parallel"),
    )(bias, proj.reshape(bsz, 1, -1), kv_new.reshape(bsz, 1, chans), *windows)
    return out.reshape(bsz, ATTN_WIDTH)


def _alibi_slopes():
    n = N_GROUPS * N_HEADS
    e = jnp.arange(1, n + 1, dtype=jnp.float32)
    return jnp.exp2(-ALIBI_MAX_EXP * e / n)


def _ssm_layer(x, h0_re, h0_im, layer, norm_pre, w_in, ssm_w, w_glu, b_glu, w_out, norm_post, *, n_steps,
               cache_job=None):
    uz = norm_matmul(x, norm_pre, w_in, layer)
    g, h_re, h_im = ssm_core(uz, h0_re.shape[0], h0_re=h0_re, h0_im=h0_im, n_steps=n_steps, **ssm_w)
    shifted = None
    if cache_job is None:
        h2 = glu(g, w_glu, layer, b_glu.reshape(1, -1), uz)
    else:
        h2, *shifted = glu_and_cache_shift(g, w_glu, layer, b_glu.reshape(1, -1), uz, *cache_job)
    return out_proj(h2, w_out, layer, norm_post, x), h_re, h_im, shifted


def _states_out(h, bsz):
    return h.reshape(bsz, SSM_GROUPS, SSM_STATE)


def kernel(x_prompt, x_sample, state_s5_re, state_s5_im, cache_kv, a_norm_pre, a_w_in, a_lam_re, a_lam_im, a_log_dt, a_b_re, a_b_im, a_c_re, a_c_im, a_d, a_w_glu, a_b_glu, a_w_out, a_norm_post, kv_norm, w_kv, b_norm_pre, b_w_in, b_w_out, b_norm_post):
    bf16 = jnp.bfloat16
    n_a = a_w_in.shape[0]
    n_b = b_w_in.shape[0]
    bsz, seq, _ = x_prompt.shape
    dbsz = x_sample.shape[0]
    n_pos = cache_kv.shape[1]

    a_w_in_b = a_w_in.astype(bf16)
    a_w_glu_b = a_w_glu.astype(bf16)
    a_w_out_b = a_w_out.astype(bf16)
    w_kv_b = w_kv.astype(bf16)
    b_w_in_b = b_w_in.astype(bf16)
    b_w_out_b = b_w_out.astype(bf16)
    slopes = _alibi_slopes()
    ssm_ws = [ssm_weights(a_lam_re[l], a_lam_im[l], a_log_dt[l], a_b_re[l], a_b_im[l],
                           a_c_re[l], a_c_im[l], a_d[l]) for l in range(n_a)]

    xs = x_sample.reshape(dbsz, D_MODEL)
    s_re, s_im = [], []
    for l in range(n_a):
        h0_re = state_s5_re[l].reshape(dbsz, SSM_GROUPS * SSM_STATE)
        h0_im = state_s5_im[l].reshape(dbsz, SSM_GROUPS * SSM_STATE)
        xs, h_re, h_im, _ = _ssm_layer(xs, h0_re, h0_im, l, a_norm_pre[l], a_w_in_b, ssm_ws[l], a_w_glu_b,
                                       a_b_glu[l], a_w_out_b, a_norm_post[l], n_steps=1)
        s_re.append(_states_out(h_re, dbsz))
        s_im.append(_states_out(h_im, dbsz))
    kv_s, kv_s_t = kv_project(xs[None], kv_norm, w_kv_b)
    cache_t = jnp.transpose(cache_kv, (0, 2, 3, 4, 1)).reshape(dbsz, 2 * ATTN_WIDTH, n_pos)

    x = jnp.swapaxes(x_prompt, 0, 1).reshape(seq * bsz, D_MODEL)
    zeros = jnp.zeros((bsz, SSM_GROUPS * SSM_STATE), jnp.float32)
    p_re, p_im = [], []
    shifted = None
    for l in range(n_a):
        cache_job = (cache_t, kv_s_t[0], l * (dbsz // n_a), shifted)
        x, h_re, h_im, shifted = _ssm_layer(x, zeros, zeros, l, a_norm_pre[l], a_w_in_b, ssm_ws[l], a_w_glu_b,
                                            a_b_glu[l], a_w_out_b, a_norm_post[l], n_steps=128,
                                            cache_job=cache_job)
        p_re.append(_states_out(h_re, bsz))
        p_im.append(_states_out(h_im, bsz))
    new_cache_t, *windows = shifted
    sample_kv = jnp.transpose(new_cache_t.reshape(dbsz, 2, N_HEADS, HEAD_DIM, n_pos), (0, 4, 1, 2, 3))
    x = jnp.swapaxes(x.reshape(seq, bsz, D_MODEL), 0, 1)
    kv, kv_t = kv_project(x, kv_norm, w_kv_b)
    x = x.reshape(bsz * seq, D_MODEL)
    for l in range(n_b):
        proj = norm_matmul(x, b_norm_pre[l], b_w_in_b, l)
        o = prompt_attention(slopes, proj.reshape(bsz, seq, -1), kv)
        x = out_proj(o.reshape(bsz * seq, ATTN_WIDTH), b_w_out_b, l, b_norm_post[l], x)
    y_prompt = x.reshape(bsz, seq, D_MODEL)
    prompt_kv = jnp.transpose(kv_t.reshape(bsz, 2, N_HEADS, HEAD_DIM, seq), (0, 4, 1, 2, 3))

    for l in range(n_b):
        proj = norm_matmul(xs, b_norm_pre[l], b_w_in_b, l)
        o = sample_attention(slopes, proj, kv_s[0], windows)
        xs = gated_out_proj(o, proj, N_GROUPS, b_w_out_b, l, b_norm_post[l], xs)
    y_sample = xs.reshape(dbsz, 1, D_MODEL)

    return (y_prompt, y_sample, jnp.stack(p_re), jnp.stack(p_im), prompt_kv,
            jnp.stack(s_re), jnp.stack(s_im), sample_kv)
```

```python
import functools
import math

import jax
import jax.numpy as jnp
from jax import lax
from jax.experimental import pallas as pl
from jax.experimental.pallas import tpu as pltpu

D_MODEL = 1024
SSM_WIDTH = 2048
SSM_GROUP = 16
SSM_GROUPS = SSM_WIDTH // SSM_GROUP
SSM_STATE = 64
HEAD_DIM = 64
N_HEADS = 16
ATTN_WIDTH = N_HEADS * HEAD_DIM
DILATIONS = (1, 4, 16)
N_STRIDES = 128
N_GROUPS = len(DILATIONS)
ALIBI_MAX_EXP = 8.0
RMS_EPS = 1e-6
NEG_INF = -1e30

VMEM_LIMIT_BYTES_V7X = 56 * 1024 * 1024
LANES = 128

GROUPS_PER_BLOCK = 16
N_GROUP_BLOCKS = SSM_GROUPS // GROUPS_PER_BLOCK
BLOCK_CHANNELS = GROUPS_PER_BLOCK * SSM_GROUP
BLOCK_STATES = GROUPS_PER_BLOCK * SSM_STATE
SSM_STAGES = 2

ROW_TILE = 512


def _params(*semantics):
    return pltpu.CompilerParams(dimension_semantics=semantics,
                                vmem_limit_bytes=VMEM_LIMIT_BYTES_V7X)


def _rms_scale(x):
    return lax.rsqrt(jnp.mean(x * x, axis=-1, keepdims=True) + RMS_EPS)


def _norm_matmul_kernel(x_ref, g_ref, w_ref, o_ref):
    x = x_ref[...]
    xn = (x * _rms_scale(x) * g_ref[...]).astype(jnp.bfloat16)
    o_ref[...] = jnp.dot(xn, w_ref[...], preferred_element_type=jnp.float32)


def norm_matmul(x, g, w, layer):
    rows, d = x.shape
    n = w.shape[2]
    tm = min(ROW_TILE, rows)
    return pl.pallas_call(
        _norm_matmul_kernel,
        grid=(rows // tm,),
        in_specs=[
            pl.BlockSpec((tm, d), lambda i: (i, 0)),
            pl.BlockSpec((1, d), lambda i: (0, 0)),
            pl.BlockSpec((None, d, n), lambda i: (layer, 0, 0)),
        ],
        out_specs=pl.BlockSpec((tm, n), lambda i: (i, 0)),
        out_shape=jax.ShapeDtypeStruct((rows, n), jnp.float32),
        compiler_params=_params("parallel"),
    )(x, g.reshape(1, d), w)


def _kv_kernel(x_ref, g_ref, w_ref, kv_ref, kvt_ref):
    x = x_ref[...]
    xn = (x * _rms_scale(x) * g_ref[...]).astype(jnp.bfloat16)
    kv = jnp.dot(xn, w_ref[...], preferred_element_type=jnp.float32)
    kv_ref[...] = kv
    kvt_ref[...] = kv.T


def kv_project(x, g, w, *, tm=256):
    bsz, seq, d = x.shape
    n = w.shape[1]
    tm = min(tm, seq)
    return pl.pallas_call(
        _kv_kernel,
        grid=(bsz, seq // tm),
        in_specs=[
            pl.BlockSpec((None, tm, d), lambda b, i: (b, i, 0)),
            pl.BlockSpec((1, d), lambda b, i: (0, 0)),
            pl.BlockSpec((d, n), lambda b, i: (0, 0)),
        ],
        out_specs=[
            pl.BlockSpec((None, tm, n), lambda b, i: (b, i, 0)),
            pl.BlockSpec((None, n, tm), lambda b, i: (b, 0, i)),
        ],
        out_shape=[
            jax.ShapeDtypeStruct((bsz, seq, n), jnp.float32),
            jax.ShapeDtypeStruct((bsz, n, seq), jnp.float32),
        ],
        compiler_params=_params("parallel", "parallel"),
    )(x, g.reshape(1, d), w)


def _ssm_weights_kernel(lam_re_ref, lam_im_ref, log_dt_ref, bt_re_ref, bt_im_ref, ct_re_ref, ct_im_ref,
                        ab_re_ref, ab_im_ref, bre_ref, bim_ref, cre_ref, cim_ref):
    lam_re = lam_re_ref[...]
    lam_im = lam_im_ref[...]
    dt = jnp.exp(log_dt_ref[...])
    mag = jnp.exp(lam_re * dt)
    ab_re = mag * jnp.cos(lam_im * dt)
    ab_im = mag * jnp.sin(lam_im * dt)
    den = lam_re * lam_re + lam_im * lam_im
    num_re = ab_re - 1.0
    coef_re = (num_re * lam_re + ab_im * lam_im) / den
    coef_im = (ab_im * lam_re - num_re * lam_im) / den
    ab_re_ref[...] = ab_re
    ab_im_ref[...] = ab_im
    b_re = bt_re_ref[...]
    b_im = bt_im_ref[...]
    cr = coef_re[:, None, :]
    ci = coef_im[:, None, :]
    bbt_re = cr * b_re - ci * b_im
    bbt_im = cr * b_im + ci * b_re

    bf16 = jnp.bfloat16

    def block_diag(w, rows, cols, rows_per_group, cols_per_group):
        src = lax.broadcasted_iota(jnp.int32, (cols_per_group, cols), 0)
        dst = lax.broadcasted_iota(jnp.int32, (cols_per_group, cols), 1)
        spread = jnp.where((dst & (cols_per_group - 1)) == src, 1.0, 0.0).astype(bf16)
        tiled = jnp.dot(w.astype(bf16), spread, preferred_element_type=jnp.float32)
        r = lax.broadcasted_iota(jnp.int32, (rows, cols), 0) // rows_per_group
        c = lax.broadcasted_iota(jnp.int32, (rows, cols), 1) // cols_per_group
        return jnp.where(r == c, tiled, 0.0).astype(bf16)

    gc, p = b_re.shape[1], b_re.shape[2]
    bre_ref[...] = block_diag(bbt_re.reshape(BLOCK_CHANNELS, p), BLOCK_CHANNELS, BLOCK_STATES, gc, p)
    bim_ref[...] = block_diag(bbt_im.reshape(BLOCK_CHANNELS, p), BLOCK_CHANNELS, BLOCK_STATES, gc, p)
    cre_ref[...] = block_diag(ct_re_ref[...].reshape(BLOCK_STATES, gc), BLOCK_STATES, BLOCK_CHANNELS, p, gc)
    cim_ref[...] = block_diag(ct_im_ref[...].reshape(BLOCK_STATES, gc), BLOCK_STATES, BLOCK_CHANNELS, p, gc)


def ssm_weights(lam_re, lam_im, log_dt, b_re, b_im, c_re, c_im, d_skip):
    g, p = lam_re.shape
    gc = b_re.shape[-1]
    gpb = GROUPS_PER_BLOCK
    f32, bf16 = jnp.float32, jnp.bfloat16
    grp2 = lambda w: pl.BlockSpec((gpb, w), lambda i: (i, 0))
    grp3 = lambda a, b: pl.BlockSpec((gpb, a, b), lambda i: (i, 0, 0))
    blk = lambda a, b: pl.BlockSpec((None, a, b), lambda i: (i, 0, 0))
    ab_re, ab_im, bre, bim, cre, cim = pl.pallas_call(
        _ssm_weights_kernel,
        grid=(N_GROUP_BLOCKS,),
        in_specs=[grp2(p), grp2(p), grp2(1), grp3(gc, p), grp3(gc, p), grp3(p, gc), grp3(p, gc)],
        out_specs=[grp2(p), grp2(p), blk(BLOCK_CHANNELS, BLOCK_STATES), blk(BLOCK_CHANNELS, BLOCK_STATES),
                   blk(BLOCK_STATES, BLOCK_CHANNELS), blk(BLOCK_STATES, BLOCK_CHANNELS)],
        out_shape=[jax.ShapeDtypeStruct((g, p), f32), jax.ShapeDtypeStruct((g, p), f32),
                   jax.ShapeDtypeStruct((N_GROUP_BLOCKS, BLOCK_CHANNELS, BLOCK_STATES), bf16),
                   jax.ShapeDtypeStruct((N_GROUP_BLOCKS, BLOCK_CHANNELS, BLOCK_STATES), bf16),
                   jax.ShapeDtypeStruct((N_GROUP_BLOCKS, BLOCK_STATES, BLOCK_CHANNELS), bf16),
                   jax.ShapeDtypeStruct((N_GROUP_BLOCKS, BLOCK_STATES, BLOCK_CHANNELS), bf16)],
        compiler_params=_params("parallel"),
    )(lam_re, lam_im, log_dt.reshape(g, 1), jnp.swapaxes(b_re, 1, 2), jnp.swapaxes(b_im, 1, 2),
      jnp.swapaxes(c_re, 1, 2), jnp.swapaxes(c_im, 1, 2))
    return dict(bre=bre, bim=bim, cre=cre, cim=cim,
                ar=ab_re.reshape(N_GROUP_BLOCKS, 1, BLOCK_STATES), ai=ab_im.reshape(N_GROUP_BLOCKS, 1, BLOCK_STATES),
                d_skip=d_skip.reshape(N_GROUP_BLOCKS, 1, BLOCK_CHANNELS))


def _ssm_kernel(u_ref, bre_ref, bim_ref, cre_ref, cim_ref, ar_ref, ai_ref, d_ref,
                h0re_ref, h0im_ref, g_ref, hre_out_ref, him_out_ref,
                sre_ref, sim_ref, hre_ref, him_ref, *, n_steps, bsz):
    tc = pl.program_id(1)

    @pl.when(tc == 0)
    def _():
        hre_ref[...] = h0re_ref[...]
        him_ref[...] = h0im_ref[...]

    ar = jnp.broadcast_to(ar_ref[...], (bsz, BLOCK_STATES))
    ai = jnp.broadcast_to(ai_ref[...], (bsz, BLOCK_STATES))

    n_stages = min(SSM_STAGES, n_steps)
    steps_per_stage = n_steps // n_stages
    stage_rows = lambda s: slice(s * steps_per_stage * bsz, (s + 1) * steps_per_stage * bsz)

    def project_in(s):
        ub = u_ref[stage_rows(s), :].astype(jnp.bfloat16)
        sre_ref[stage_rows(s), :] = jnp.dot(ub, bre_ref[...], preferred_element_type=jnp.float32)
        sim_ref[stage_rows(s), :] = jnp.dot(ub, bim_ref[...], preferred_element_type=jnp.float32)

    def recur(s, hr, hi):
        for t in range(s * steps_per_stage, (s + 1) * steps_per_stage):
            rows = slice(t * bsz, (t + 1) * bsz)
            nhr = ar * hr - ai * hi + sre_ref[rows, :]
            nhi = ar * hi + ai * hr + sim_ref[rows, :]
            sre_ref[rows, :] = nhr
            sim_ref[rows, :] = nhi
            hr, hi = nhr, nhi
        return hr, hi

    def project_out(s):
        rows = stage_rows(s)
        y = jnp.dot(sre_ref[rows, :].astype(jnp.bfloat16), cre_ref[...], preferred_element_type=jnp.float32)
        y = y - jnp.dot(sim_ref[rows, :].astype(jnp.bfloat16), cim_ref[...], preferred_element_type=jnp.float32)
        y = y + d_ref[...] * u_ref[rows, :]
        g_ref[rows, :] = jax.nn.gelu(y).astype(jnp.bfloat16)

    hr, hi = hre_ref[...], him_ref[...]
    project_in(0)
    for s in range(n_stages):
        if s + 1 < n_stages:
            project_in(s + 1)
        hr, hi = recur(s, hr, hi)
        project_out(s)
    hre_ref[...] = hr
    him_ref[...] = hi

    @pl.when(tc == pl.num_programs(1) - 1)
    def _():
        hre_out_ref[...] = hr
        him_out_ref[...] = hi


def ssm_core(uz, bsz, bre, bim, cre, cim, ar, ai, d_skip, h0_re, h0_im, *, n_steps):
    rows = uz.shape[0]
    seq = rows // bsz
    n_steps = min(n_steps, seq)
    tr = n_steps * bsz
    kern = functools.partial(_ssm_kernel, n_steps=n_steps, bsz=bsz)
    f32 = jnp.float32
    wspec = lambda shape: pl.BlockSpec((None,) + shape, lambda gb, tc: (gb, 0, 0))
    return pl.pallas_call(
        kern,
        grid=(N_GROUP_BLOCKS, seq // n_steps),
        in_specs=[
            pl.BlockSpec((tr, BLOCK_CHANNELS), lambda gb, tc: (tc, gb)),
            wspec((BLOCK_CHANNELS, BLOCK_STATES)), wspec((BLOCK_CHANNELS, BLOCK_STATES)),
            wspec((BLOCK_STATES, BLOCK_CHANNELS)), wspec((BLOCK_STATES, BLOCK_CHANNELS)),
            wspec((1, BLOCK_STATES)), wspec((1, BLOCK_STATES)), wspec((1, BLOCK_CHANNELS)),
            pl.BlockSpec((bsz, BLOCK_STATES), lambda gb, tc: (0, gb)),
            pl.BlockSpec((bsz, BLOCK_STATES), lambda gb, tc: (0, gb)),
        ],
        out_specs=[
            pl.BlockSpec((tr, BLOCK_CHANNELS), lambda gb, tc: (tc, gb)),
            pl.BlockSpec((bsz, BLOCK_STATES), lambda gb, tc: (0, gb)),
            pl.BlockSpec((bsz, BLOCK_STATES), lambda gb, tc: (0, gb)),
        ],
        out_shape=[
            jax.ShapeDtypeStruct((rows, SSM_WIDTH), jnp.bfloat16),
            jax.ShapeDtypeStruct((bsz, SSM_GROUPS * SSM_STATE), f32),
            jax.ShapeDtypeStruct((bsz, SSM_GROUPS * SSM_STATE), f32),
        ],
        scratch_shapes=[pltpu.VMEM((tr, BLOCK_STATES), f32), pltpu.VMEM((tr, BLOCK_STATES), f32),
                        pltpu.VMEM((bsz, BLOCK_STATES), f32), pltpu.VMEM((bsz, BLOCK_STATES), f32)],
        compiler_params=_params("parallel", "arbitrary"),
    )(uz, bre, bim, cre, cim, ar, ai, d_skip, h0_re, h0_im)


def _glu_kernel(g_ref, wa_ref, wb_ref, ba_ref, bb_ref, z_ref, o_ref):
    g = g_ref[...]
    a = jnp.dot(g, wa_ref[...], preferred_element_type=jnp.float32) + ba_ref[...]
    b = jnp.dot(g, wb_ref[...], preferred_element_type=jnp.float32) + bb_ref[...]
    o_ref[...] = (a * jax.nn.sigmoid(b) * jax.nn.silu(z_ref[...])).astype(jnp.bfloat16)


def glu(g, w_glu, layer, b_glu, uz, *, tn=512):
    rows, e = g.shape
    tm = min(ROW_TILE, rows)
    nj = e // tn
    return pl.pallas_call(
        _glu_kernel,
        grid=(nj, rows // tm),
        in_specs=[
            pl.BlockSpec((tm, e), lambda j, i: (i, 0)),
            pl.BlockSpec((None, e, tn), lambda j, i: (layer, 0, j)),
            pl.BlockSpec((None, e, tn), lambda j, i: (layer, 0, nj + j)),
            pl.BlockSpec((1, tn), lambda j, i: (0, j)),
            pl.BlockSpec((1, tn), lambda j, i: (0, nj + j)),
            pl.BlockSpec((tm, tn), lambda j, i: (i, nj + j)),
        ],
        out_specs=pl.BlockSpec((tm, tn), lambda j, i: (i, j)),
        out_shape=jax.ShapeDtypeStruct((rows, e), jnp.bfloat16),
        compiler_params=_params("parallel", "parallel"),
    )(g, w_glu, w_glu, b_glu, b_glu, uz)


def _out_proj_kernel(h_ref, w_ref, g_ref, x_ref, o_ref):
    y = jnp.dot(h_ref[...], w_ref[...], preferred_element_type=jnp.float32)
    o_ref[...] = x_ref[...] + y * _rms_scale(y) * g_ref[...]


def _gated_out_proj_kernel(a_ref, z_ref, w_ref, g_ref, x_ref, o_ref):
    h = (a_ref[...] * jax.nn.silu(z_ref[...])).astype(jnp.bfloat16)
    y = jnp.dot(h, w_ref[...], preferred_element_type=jnp.float32)
    o_ref[...] = x_ref[...] + y * _rms_scale(y) * g_ref[...]


def out_proj(h, w, layer, g, x):
    rows, d = x.shape
    k = w.shape[1]
    tm = min(ROW_TILE, rows)
    return pl.pallas_call(
        _out_proj_kernel,
        grid=(rows // tm,),
        in_specs=[
            pl.BlockSpec((tm, k), lambda i: (i, 0)),
            pl.BlockSpec((None, k, d), lambda i: (layer, 0, 0)),
            pl.BlockSpec((1, d), lambda i: (0, 0)),
            pl.BlockSpec((tm, d), lambda i: (i, 0)),
        ],
        out_specs=pl.BlockSpec((tm, d), lambda i: (i, 0)),
        out_shape=jax.ShapeDtypeStruct((rows, d), jnp.float32),
        compiler_params=_params("parallel"),
    )(h, w, g.reshape(1, d), x)


def gated_out_proj(a, proj, z_block, w, layer, g, x):
    rows, d = x.shape
    k = w.shape[1]
    return pl.pallas_call(
        _gated_out_proj_kernel,
        grid=(1,),
        in_specs=[
            pl.BlockSpec((rows, k), lambda i: (0, 0)),
            pl.BlockSpec((rows, k), lambda i: (0, z_block)),
            pl.BlockSpec((None, k, d), lambda i: (layer, 0, 0)),
            pl.BlockSpec((1, d), lambda i: (0, 0)),
            pl.BlockSpec((rows, d), lambda i: (0, 0)),
        ],
        out_specs=pl.BlockSpec((rows, d), lambda i: (0, 0)),
        out_shape=jax.ShapeDtypeStruct((rows, d), jnp.float32),
        compiler_params=_params("arbitrary"),
    )(a, proj, w, g.reshape(1, d), x)


QB = N_STRIDES
HEAD_PAIR = 2 * HEAD_DIM
LOG2_E = math.log2(math.e)
ATTN_UNROLL = 16


def _prompt_attn_kernel(slopes_ref, q0_ref, q1_ref, q2_ref, z_ref, k_ref, v_ref, o_ref,
                        acc0, acc1, acc2, m0, m1, m2, l0, l1, l2, va0_ref, va1_ref, bias_ref, *, seq):
    hp = pl.program_id(1)
    q_refs = (q0_ref, q1_ref, q2_ref)
    accs, ms, ls = (acc0, acc1, acc2), (m0, m1, m2), (l0, l1, l2)
    first_head = lax.broadcasted_iota(jnp.int32, (QB, HEAD_PAIR), 1) < HEAD_DIM
    nt = (((1,), (1,)), ((), ()))
    blocks_per_group = seq // QB
    bf16 = jnp.bfloat16
    q_scale = LOG2_E / math.sqrt(HEAD_DIM)

    def fill_values(i, _):
        rows = pl.ds(pl.multiple_of(i * QB, QB), QB)
        v = v_ref[rows, :]
        va0_ref[rows, :] = jnp.where(first_head, v, 1.0)
        va1_ref[rows, :] = jnp.where(first_head, 1.0, v)
        return 0

    lax.fori_loop(0, seq // QB, fill_values, 0, unroll=2)

    for g, dil in enumerate(DILATIONS):
        blocks_per_res = seq // dil // QB
        has_prev_block = blocks_per_res > 1
        n_keys = 2 * QB if has_prev_block else QB

        qi = lax.broadcasted_iota(jnp.int32, (QB, n_keys), 0)
        kj = lax.broadcasted_iota(jnp.int32, (QB, n_keys), 1)
        dist = qi + (n_keys - QB) - kj
        valid = jnp.logical_and(dist >= 0, dist <= N_STRIDES)
        distf = dist.astype(jnp.float32)
        for hh in range(2):
            slope = slopes_ref[g * N_HEADS + 2 * hp + hh] * (float(dil) * LOG2_E)
            table = jnp.where(valid, -slope * distf, NEG_INF)
            bias_ref[1, hh * QB:(hh + 1) * QB, 0:n_keys] = table
            if has_prev_block:
                bias_ref[0, hh * QB:(hh + 1) * QB, 0:n_keys] = jnp.where(kj >= QB, table, NEG_INF)

        def body(j, _, g=g, dil=dil, has_prev_block=has_prev_block, n_keys=n_keys):
            if dil == 1:
                start = pl.multiple_of(j * QB, QB)
                prev_start = jnp.maximum(start - QB, 0)
                rows = lambda s: pl.ds(s, QB)
                has_prev = jnp.minimum(j, 1)
            else:
                blk = j // dil
                res = j - blk * dil
                start = res + blk * (QB * dil)
                prev_start = jnp.maximum(start - QB * dil, 0)
                rows = lambda s: pl.ds(s, QB, stride=dil)
                has_prev = jnp.minimum(blk, 1)
            q2 = q_refs[g][rows(start), :] * q_scale
            qs = jnp.concatenate([jnp.where(first_head, q2, 0.0), jnp.where(first_head, 0.0, q2)], axis=0)
            if has_prev_block:
                kk = jnp.concatenate([k_ref[rows(prev_start), :], k_ref[rows(start), :]], axis=0)
                v_aug = jnp.concatenate(
                    [jnp.concatenate([va0_ref[rows(prev_start), :], va0_ref[rows(start), :]], axis=0),
                     jnp.concatenate([va1_ref[rows(prev_start), :], va1_ref[rows(start), :]], axis=0)], axis=1)
                bias = bias_ref[has_prev]
            else:
                kk = k_ref[rows(start), :]
                v_aug = jnp.concatenate([va0_ref[rows(start), :], va1_ref[rows(start), :]], axis=1)
                bias = bias_ref[1, :, 0:n_keys]
            s = lax.dot_general(qs.astype(bf16), kk.astype(bf16), nt, preferred_element_type=jnp.float32)
            s = s + bias
            mx = jnp.max(s, axis=-1, keepdims=True)
            p = jnp.exp2(s - mx).astype(bf16)
            r = jnp.dot(p, v_aug.astype(bf16), preferred_element_type=jnp.float32)
            r0 = r[:QB, :HEAD_PAIR]
            r1 = r[QB:, HEAD_PAIR:]
            accs[g][rows(start), :] = jnp.where(first_head, r0, r1)
            ls[g][rows(start), :] = pltpu.roll(jnp.where(first_head, r1, r0), HEAD_DIM, 1)
            ms[g][rows(start), :] = jnp.where(first_head, jnp.broadcast_to(mx[:QB], (QB, HEAD_PAIR)),
                                              jnp.broadcast_to(mx[QB:], (QB, HEAD_PAIR)))
            return 0

        lax.fori_loop(0, blocks_per_group, body, 0, unroll=ATTN_UNROLL)

    def merge(i, _):
        rows = pl.ds(pl.multiple_of(i * QB, QB), QB)
        mm = jnp.maximum(jnp.maximum(m0[rows, :], m1[rows, :]), m2[rows, :])
        num = jnp.zeros((QB, HEAD_PAIR), jnp.float32)
        den = jnp.zeros((QB, HEAD_PAIR), jnp.float32)
        for g in range(N_GROUPS):
            e = jnp.exp2(ms[g][rows, :] - mm)
            num = num + e * accs[g][rows, :]
            den = den + e * ls[g][rows, :]
        o_ref[rows, :] = ((num / den) * jax.nn.silu(z_ref[rows, :])).astype(jnp.bfloat16)
        return 0

    lax.fori_loop(0, seq // QB, merge, 0, unroll=2)


def prompt_attention(slopes, proj, kv):
    bsz, seq, _ = proj.shape
    n_pairs = ATTN_WIDTH // HEAD_PAIR
    col = lambda off: pl.BlockSpec((None, seq, HEAD_PAIR), lambda b, hp, off=off: (b, 0, off + hp))
    f32 = jnp.float32
    scratch = [pltpu.VMEM((seq, HEAD_PAIR), f32) for _ in range(11)]
    scratch.append(pltpu.VMEM((2, 2 * QB, 2 * QB), f32))
    return pl.pallas_call(
        functools.partial(_prompt_attn_kernel, seq=seq),
        grid=(bsz, n_pairs),
        in_specs=[pl.BlockSpec(memory_space=pltpu.SMEM),
                  col(0), col(n_pairs), col(2 * n_pairs), col(3 * n_pairs), col(0), col(n_pairs)],
        out_specs=pl.BlockSpec((None, seq, HEAD_PAIR), lambda b, hp: (b, 0, hp)),
        out_shape=jax.ShapeDtypeStruct((bsz, seq, ATTN_WIDTH), jnp.bfloat16),
        scratch_shapes=scratch,
        compiler_params=_params("parallel", "parallel"),
    )(slopes, proj, proj, proj, proj, kv, kv)


def _window_selector(n_pos):
    pos = jnp.arange(n_pos)[:, None]
    j = jnp.arange(LANES)[None, :]
    cols = [pos == n_pos - N_STRIDES * dil + dil * j for dil in DILATIONS if dil > 1]
    return jnp.concatenate(cols, axis=1).astype(jnp.bfloat16)


def _cache_shift_kernel(b, c_ref, new_ref, sel_ref, o_ref, *win_refs):
    rows, n_pos = c_ref.shape
    x = c_ref[...]
    o_ref[...] = pltpu.roll(x, n_pos - 1, 1)
    new = new_ref[...]
    lane = lax.broadcasted_iota(jnp.int32, new.shape, 1)
    new_col = jnp.sum(jnp.where(lane == b, new, 0.0), axis=1, keepdims=True)
    last = o_ref[:, n_pos - LANES:]
    last_lane = lax.broadcasted_iota(jnp.int32, last.shape, 1) == LANES - 1
    o_ref[:, n_pos - LANES:] = jnp.where(last_lane, new_col, last)
    picked = jnp.dot(x.astype(jnp.bfloat16), sel_ref[...], preferred_element_type=jnp.float32)
    k = 0
    for ref, dil in zip(win_refs, DILATIONS):
        if dil == 1:
            ref[...] = x[:, n_pos - LANES:].astype(jnp.bfloat16)
        else:
            ref[...] = picked[:, k * LANES:(k + 1) * LANES].astype(jnp.bfloat16)
            k += 1


SHIFT_ROWS = 1024
GLU_SHIFT_ROW_TILE = 512


def _glu_shift_kernel(g_ref, wa_ref, wb_ref, ba_ref, bb_ref, z_ref, c_ref, new_ref, sel_ref, *refs,
                      n_aliased, batch_offset, row_tiles, blocks_per_seq):
    o_ref, oc_ref, *win_refs = refs[n_aliased:]
    _glu_kernel(g_ref, wa_ref, wb_ref, ba_ref, bb_ref, z_ref, o_ref)
    step = pl.program_id(0) * row_tiles + pl.program_id(1)
    _cache_shift_kernel(batch_offset + step // blocks_per_seq, c_ref, new_ref, sel_ref, oc_ref, *win_refs)


def glu_and_cache_shift(g, w_glu, layer, b_glu, uz, cache_t, new_t, batch_offset, shifted=None, *, tn=512):
    rows, e = g.shape
    bsz, chans, n_pos = cache_t.shape
    tm = GLU_SHIFT_ROW_TILE
    nj, ni = e // tn, rows // tm
    bps = chans // SHIFT_ROWS
    assert (nj * ni) % bps == 0 and batch_offset + (nj * ni) // bps <= bsz
    n_win = len(DILATIONS)
    seq_of = lambda j, i: batch_offset + (j * ni + i) // bps
    blk_of = lambda j, i: (j * ni + i) % bps
    cache_spec = pl.BlockSpec((None, SHIFT_ROWS, n_pos), lambda j, i: (seq_of(j, i), blk_of(j, i), 0))
    win_spec = pl.BlockSpec((None, SHIFT_ROWS, LANES), lambda j, i: (seq_of(j, i), blk_of(j, i), 0))
    aliased = [] if shifted is None else list(shifted)
    n_glu_in = 9
    sel = _window_selector(n_pos)
    kern = functools.partial(_glu_shift_kernel, n_aliased=len(aliased), batch_offset=batch_offset,
                             row_tiles=ni, blocks_per_seq=bps)
    return pl.pallas_call(
        kern,
        grid=(nj, ni),
        in_specs=[
            pl.BlockSpec((tm, e), lambda j, i: (i, 0)),
            pl.BlockSpec((None, e, tn), lambda j, i: (layer, 0, j)),
            pl.BlockSpec((None, e, tn), lambda j, i: (layer, 0, nj + j)),
            pl.BlockSpec((1, tn), lambda j, i: (0, j)),
            pl.BlockSpec((1, tn), lambda j, i: (0, nj + j)),
            pl.BlockSpec((tm, tn), lambda j, i: (i, nj + j)),
            cache_spec,
            pl.BlockSpec((SHIFT_ROWS, bsz), lambda j, i: (blk_of(j, i), 0)),
            pl.BlockSpec(sel.shape, lambda j, i: (0, 0)),
        ] + [pl.BlockSpec(memory_space=pl.ANY)] * len(aliased),
        out_specs=[pl.BlockSpec((tm, tn), lambda j, i: (i, j)), cache_spec] + [win_spec] * n_win,
        out_shape=[jax.ShapeDtypeStruct((rows, e), jnp.bfloat16),
                   jax.ShapeDtypeStruct(cache_t.shape, cache_t.dtype)]
                  + [jax.ShapeDtypeStruct((bsz, chans, LANES), jnp.bfloat16)] * n_win,
        input_output_aliases={n_glu_in + k: 1 + k for k in range(len(aliased))},
        compiler_params=_params("parallel", "parallel"),
    )(g, w_glu, w_glu, b_glu, b_glu, uz, cache_t, new_t, sel, *aliased)


SEQS_PER_STEP = 8


def _sample_attn_kernel(bias_ref, proj_ref, new_ref, c1_ref, c4_ref, c16_ref, o_ref):
    packed = (c1_ref, c4_ref, c16_ref)
    head_of_chan = lax.broadcasted_iota(jnp.int32, (N_HEADS, ATTN_WIDTH), 1) // HEAD_DIM
    own = head_of_chan == lax.broadcasted_iota(jnp.int32, (N_HEADS, ATTN_WIDTH), 0)
    nt = (((1,), (1,)), ((), ()))
    bf16 = jnp.bfloat16

    def one_sequence(i, _):
        proj = proj_ref[i]
        new = new_ref[i]
        k_new = new[:, :ATTN_WIDTH]
        v_new = new[:, ATTN_WIDTH:]
        scores, new_scores = [], []
        for g in range(N_GROUPS):
            q = proj[:, g * ATTN_WIDTH:(g + 1) * ATTN_WIDTH] * (1.0 / math.sqrt(HEAD_DIM))
            q_bd = jnp.where(own, q, 0.0)
            k = packed[g][i, 0:ATTN_WIDTH, :]
            s = jnp.dot(q_bd.astype(bf16), k, preferred_element_type=jnp.float32) + bias_ref[g]
            scores.append(s)
            new_scores.append(jnp.sum(q_bd * k_new, axis=1, keepdims=True))
        mx = jnp.maximum(jnp.maximum(new_scores[0], new_scores[1]), new_scores[2])
        for g in range(N_GROUPS):
            mx = jnp.maximum(mx, jnp.max(scores[g], axis=1, keepdims=True))
        p_new = sum(jnp.exp(s - mx) for s in new_scores)
        den = p_new
        o = p_new * v_new
        for g in range(N_GROUPS):
            p = jnp.exp(scores[g] - mx)
            den = den + jnp.sum(p, axis=1, keepdims=True)
            v = packed[g][i, ATTN_WIDTH:2 * ATTN_WIDTH, :]
            o = o + lax.dot_general(p.astype(bf16), v, nt, preferred_element_type=jnp.float32)
        o_ref[i] = jnp.sum(jnp.where(own, o / den, 0.0), axis=0, keepdims=True)
        return 0

    lax.fori_loop(0, proj_ref.shape[0], one_sequence, 0)


def sample_attention(slopes, proj, kv_new, windows):
    bsz, chans, _ = windows[0].shape
    nb = SEQS_PER_STEP
    steps = (N_STRIDES - jnp.arange(LANES)).astype(jnp.float32)
    dil = jnp.asarray(DILATIONS, jnp.float32)
    bias = -(slopes.reshape(N_GROUPS, N_HEADS) * dil[:, None])[:, :, None] * steps
    row3 = lambda width: pl.BlockSpec((nb, 1, width), lambda b: (b, 0, 0))
    win = pl.BlockSpec((nb, chans, LANES), lambda b: (b, 0, 0))
    out = pl.pallas_call(
        _sample_attn_kernel,
        grid=(bsz // nb,),
        in_specs=[pl.BlockSpec((N_GROUPS, N_HEADS, LANES), lambda b: (0, 0, 0)),
                  row3(proj.shape[1]), row3(chans), win, win, win],
        out_specs=row3(ATTN_WIDTH),
        out_shape=jax.ShapeDtypeStruct((bsz, 1, ATTN_WIDTH), jnp.float32),
        compiler_params=_params("parallel"),
    )(bias, proj.reshape(bsz, 1, -1), kv_new.reshape(bsz, 1, chans), *windows)
    return out.reshape(bsz, ATTN_WIDTH)


def _alibi_slopes():
    n = N_GROUPS * N_HEADS
    e = jnp.arange(1, n + 1, dtype=jnp.float32)
    return jnp.exp2(-ALIBI_MAX_EXP * e / n)


def _ssm_layer(x, h0_re, h0_im, layer, norm_pre, w_in, ssm_w, w_glu, b_glu, w_out, norm_post, *, n_steps,
               cache_job=None):
    uz = norm_matmul(x, norm_pre, w_in, layer)
    g, h_re, h_im = ssm_core(uz, h0_re.shape[0], h0_re=h0_re, h0_im=h0_im, n_steps=n_steps, **ssm_w)
    shifted = None
    if cache_job is None:
        h2 = glu(g, w_glu, layer, b_glu.reshape(1, -1), uz)
    else:
        h2, *shifted = glu_and_cache_shift(g, w_glu, layer, b_glu.reshape(1, -1), uz, *cache_job)
    return out_proj(h2, w_out, layer, norm_post, x), h_re, h_im, shifted


def _states_out(h, bsz):
    return h.reshape(bsz, SSM_GROUPS, SSM_STATE)


def kernel(x_prompt, x_sample, state_s5_re, state_s5_im, cache_kv, a_norm_pre, a_w_in, a_lam_re, a_lam_im, a_log_dt, a_b_re, a_b_im, a_c_re, a_c_im, a_d, a_w_glu, a_b_glu, a_w_out, a_norm_post, kv_norm, w_kv, b_norm_pre, b_w_in, b_w_out, b_norm_post):
    bf16 = jnp.bfloat16
    n_a = a_w_in.shape[0]
    n_b = b_w_in.shape[0]
    bsz, seq, _ = x_prompt.shape
    dbsz = x_sample.shape[0]
    n_pos = cache_kv.shape[1]

    a_w_in_b = a_w_in.astype(bf16)
    a_w_glu_b = a_w_glu.astype(bf16)
    a_w_out_b = a_w_out.astype(bf16)
    w_kv_b = w_kv.astype(bf16)
    b_w_in_b = b_w_in.astype(bf16)
    b_w_out_b = b_w_out.astype(bf16)
    slopes = _alibi_slopes()
    ssm_ws = [ssm_weights(a_lam_re[l], a_lam_im[l], a_log_dt[l], a_b_re[l], a_b_im[l],
                           a_c_re[l], a_c_im[l], a_d[l]) for l in range(n_a)]

    xs = x_sample.reshape(dbsz, D_MODEL)
    s_re, s_im = [], []
    for l in range(n_a):
        h0_re = state_s5_re[l].reshape(dbsz, SSM_GROUPS * SSM_STATE)
        h0_im = state_s5_im[l].reshape(dbsz, SSM_GROUPS * SSM_STATE)
        xs, h_re, h_im, _ = _ssm_layer(xs, h0_re, h0_im, l, a_norm_pre[l], a_w_in_b, ssm_ws[l], a_w_glu_b,
                                       a_b_glu[l], a_w_out_b, a_norm_post[l], n_steps=1)
        s_re.append(_states_out(h_re, dbsz))
        s_im.append(_states_out(h_im, dbsz))
    kv_s, kv_s_t = kv_project(xs[None], kv_norm, w_kv_b)
    cache_t = jnp.transpose(cache_kv, (0, 2, 3, 4, 1)).reshape(dbsz, 2 * ATTN_WIDTH, n_pos)

    x = jnp.swapaxes(x_prompt, 0, 1).reshape(seq * bsz, D_MODEL)
    zeros = jnp.zeros((bsz, SSM_GROUPS * SSM_STATE), jnp.float32)
    p_re, p_im = [], []
    shifted = None
    for l in range(n_a):
        cache_job = (cache_t, kv_s_t[0], l * (dbsz // n_a), shifted)
        x, h_re, h_im, shifted = _ssm_layer(x, zeros, zeros, l, a_norm_pre[l], a_w_in_b, ssm_ws[l], a_w_glu_b,
                                            a_b_glu[l], a_w_out_b, a_norm_post[l], n_steps=128,
                                            cache_job=cache_job)
        p_re.append(_states_out(h_re, bsz))
        p_im.append(_states_out(h_im, bsz))
    new_cache_t, *windows = shifted
    sample_kv = jnp.transpose(new_cache_t.reshape(dbsz, 2, N_HEADS, HEAD_DIM, n_pos), (0, 4, 1, 2, 3))
    x = jnp.swapaxes(x.reshape(seq, bsz, D_MODEL), 0, 1)
    kv, kv_t = kv_project(x, kv_norm, w_kv_b)
    x = x.reshape(bsz * seq, D_MODEL)
    for l in range(n_b):
        proj = norm_matmul(x, b_norm_pre[l], b_w_in_b, l)
        o = prompt_attention(slopes, proj.reshape(bsz, seq, -1), kv)
        x = out_proj(o.reshape(bsz * seq, ATTN_WIDTH), b_w_out_b, l, b_norm_post[l], x)
    y_prompt = x.reshape(bsz, seq, D_MODEL)
    prompt_kv = jnp.transpose(kv_t.reshape(bsz, 2, N_HEADS, HEAD_DIM, seq), (0, 4, 1, 2, 3))

    for l in range(n_b):
        proj = norm_matmul(xs, b_norm_pre[l], b_w_in_b, l)
        o = sample_attention(slopes, proj, kv_s[0], windows)
        xs = gated_out_proj(o, proj, N_GROUPS, b_w_out_b, l, b_norm_post[l], xs)
    y_sample = xs.reshape(dbsz, 1, D_MODEL)

    return (y_prompt, y_sample, jnp.stack(p_re), jnp.stack(p_im), prompt_kv,
            jnp.stack(s_re), jnp.stack(s_im), sample_kv)
```
